```python
import jax, jax.numpy as jnp
from jax import lax
import numpy as np

D_MODEL = 1024
BATCH = 4
SEQ = 8192
DEPTH = 4

N_A_LAYERS = DEPTH // 2
N_B_LAYERS = DEPTH - N_A_LAYERS
D_FF = 2816
CONV_WIDTH = 3
HEAD_DIM = 64
HEADS_PER_GROUP = 8
DILATED_GROUPS = ((128, 1), (512, 4), (2048, 16))
N_GROUPS = len(DILATED_GROUPS)
N_Q_HEADS = N_GROUPS * HEADS_PER_GROUP
Q_WIDTH = N_Q_HEADS * HEAD_DIM
OUT_WIDTH = HEADS_PER_GROUP * HEAD_DIM
ROPE_DIM = HEAD_DIM // 4
ROPE_THETA = 500000.0
NORM_EPS = 1e-5
FFN_RES_WEIGHT = 0.5
N_MOD = 9

kernel_name = "hybrid_shortconv_dilated_yoco_trunk"


def rms_norm(x, g):
    xf = x.astype(jnp.float32)
    y = xf * lax.rsqrt(jnp.mean(xf * xf, axis=-1, keepdims=True) + NORM_EPS)
    return (y * g.astype(jnp.float32)).astype(x.dtype)


def modulate(h, shift, scale):
    return h * (1 + scale[:, None, :]) + shift[:, None, :]


def swiglu(h, w_in, w_out):
    a, b = jnp.split(h @ w_in, 2, axis=-1)
    return (jax.nn.silu(a) * b) @ w_out


def short_conv_mixer(h, w_in, conv_w, w_out):
    b_gate, c_gate, u = jnp.split(h @ w_in, 3, axis=-1)
    v = c_gate * u
    conv = lax.conv_general_dilated(
        v, conv_w[:, None, :], window_strides=(1,), padding=[(CONV_WIDTH - 1, 0)],
        dimension_numbers=('NWC', 'WIO', 'NWC'), feature_group_count=D_MODEL)
    return (b_gate * conv) @ w_out


def rope_tables(positions):
    inv = ROPE_THETA ** (-jnp.arange(0, ROPE_DIM, 2, dtype=jnp.float32) / ROPE_DIM)
    ang = positions.astype(jnp.float32)[..., None] * inv
    return jnp.cos(ang)[:, :, None, :], jnp.sin(ang)[:, :, None, :]


def apply_partial_rope(t, cos, sin):
    tf = t.astype(jnp.float32)
    r1 = tf[..., :ROPE_DIM // 2]
    r2 = tf[..., ROPE_DIM // 2:ROPE_DIM]
    out = jnp.concatenate([r1 * cos - r2 * sin, r2 * cos + r1 * sin, tf[..., ROPE_DIM:]], axis=-1)
    return out.astype(t.dtype)


def dilated_window_attention(q, k, v, window, dilation):
    bsz, seq, nh, hd = q.shape
    n = window // dilation
    span = n * dilation
    seq_p = -(-seq // span) * span
    pad = seq_p - seq
    m_len = seq_p // dilation
    nb = m_len // n

    def to_blocks(t):
        t = jnp.pad(t, ((0, 0), (0, pad), (0, 0), (0, 0)))
        t = t.reshape(bsz, m_len, dilation, nh, hd).transpose(0, 2, 3, 1, 4)
        return t.reshape(bsz, dilation, nh, nb, n, hd)

    def with_prev(t):
        prev = jnp.pad(t[:, :, :, :-1], ((0, 0), (0, 0), (0, 0), (1, 0), (0, 0), (0, 0)))
        return jnp.concatenate([prev, t], axis=-2)

    qb = to_blocks(q)
    kk = with_prev(to_blocks(k))
    vv = with_prev(to_blocks(v))
    s = jnp.einsum('brhiqe,brhike->brhiqk', qb, kk,
                   preferred_element_type=jnp.float32) * (hd ** -0.5)
    blk = jnp.arange(nb)[:, None, None]
    qi = jnp.arange(n)[None, :, None]
    kj = jnp.arange(2 * n)[None, None, :]
    dist = n + qi - kj
    valid = (dist >= 0) & (dist <= n) & ((blk - 1) * n + kj >= 0)
    s = jnp.where(valid, s, -jnp.inf)
    m = jnp.max(s, axis=-1, keepdims=True)
    p = jnp.exp(s - m)
    den = jnp.sum(p, axis=-1, keepdims=True)
    o = jnp.einsum('brhiqk,brhike->brhiqe', (p / den).astype(v.dtype), vv)
    lse = (m + jnp.log(den))[..., 0]
    o = o.reshape(bsz, dilation, nh, m_len, hd).transpose(0, 3, 1, 2, 4).reshape(bsz, seq_p, nh, hd)
    lse = lse.reshape(bsz, dilation, nh, m_len).transpose(0, 3, 1, 2).reshape(bsz, seq_p, nh)
    return o[:, :seq], lse[:, :seq]


def dilated_attention_mixer(h, w_q, w_o, k_sh, v_sh, cos, sin):
    bsz, seq, _ = h.shape
    q = apply_partial_rope((h @ w_q).reshape(bsz, seq, N_Q_HEADS, HEAD_DIM), cos, sin)
    outs, lses = [], []
    for g, (win, dil) in enumerate(DILATED_GROUPS):
        sl = slice(g * HEADS_PER_GROUP, (g + 1) * HEADS_PER_GROUP)
        o, l = dilated_window_attention(q[:, :, sl], k_sh[:, :, sl], v_sh[:, :, sl], win, dil)
        outs.append(o)
        lses.append(l)
    o = jnp.stack(outs, axis=0).astype(jnp.float32)
    w = jax.nn.softmax(jnp.stack(lses, axis=0), axis=0)
    mixed = jnp.sum(w[..., None] * o, axis=0).astype(h.dtype)
    return mixed.reshape(bsz, seq, OUT_WIDTH) @ w_o


def shared_kv(x, g, shift, scale, w_kv, cos, sin):
    bsz, seq, _ = x.shape
    h = modulate(rms_norm(x, g), shift, scale)
    k, v = jnp.split(h @ w_kv, 2, axis=-1)
    k = apply_partial_rope(k.reshape(bsz, seq, N_Q_HEADS, HEAD_DIM), cos, sin)
    v = v.reshape(bsz, seq, N_Q_HEADS, HEAD_DIM)
    return k, v


def setup_inputs(seed: int = 0) -> dict:
    key = jax.random.key(seed)
    ks = jax.random.split(key, 24)
    f32 = jnp.float32
    D, F = D_MODEL, D_FF

    def nrm(k, shape, fan_in, mult=1.0):
        return jax.random.normal(k, shape, f32) * (mult * fan_in ** -0.5)

    x = jax.random.normal(ks[0], (BATCH, SEQ, D), f32)
    c = jax.random.normal(ks[1], (BATCH, D), f32)
    offset = jax.random.randint(ks[2], (BATCH, 1), 0, 1024, dtype=jnp.int32)
    positions = offset + jnp.arange(SEQ, dtype=jnp.int32)[None, :]
    return {
        "x": x,
        "c": c,
        "positions": positions,
        "norm_g": 1.0 + 0.02 * jax.random.normal(ks[3], (DEPTH, 3, D), f32),
        "ada_w": nrm(ks[4], (DEPTH, D, N_MOD * D), D, 0.1),
        "ada_b": 0.01 * jax.random.normal(ks[5], (DEPTH, N_MOD * D), f32),
        "ffn1_w_in": nrm(ks[6], (DEPTH, D, 2 * F), D),
        "ffn1_w_out": nrm(ks[7], (DEPTH, F, D), F),
        "ffn2_w_in": nrm(ks[8], (DEPTH, D, 2 * F), D),
        "ffn2_w_out": nrm(ks[9], (DEPTH, F, D), F),
        "conv_w_in": nrm(ks[10], (N_A_LAYERS, D, 3 * D), D),
        "conv_w": nrm(ks[11], (N_A_LAYERS, CONV_WIDTH, D), CONV_WIDTH),
        "conv_w_out": nrm(ks[12], (N_A_LAYERS, D, D), D),
        "kv_norm_g": 1.0 + 0.02 * jax.random.normal(ks[13], (D,), f32),
        "kv_ada_w": nrm(ks[14], (D, 2 * D), D, 0.1),
        "kv_ada_b": 0.01 * jax.random.normal(ks[15], (2 * D,), f32),
        "w_kv": nrm(ks[16], (D, 2 * Q_WIDTH), D),
        "attn_w_q": nrm(ks[17], (N_B_LAYERS, D, Q_WIDTH), D),
        "attn_w_o": nrm(ks[18], (N_B_LAYERS, OUT_WIDTH, D), OUT_WIDTH),
        "final_norm_g": 1.0 + 0.02 * jax.random.normal(ks[19], (D,), f32),
    }


def reference(x, c, positions, norm_g, ada_w, ada_b, ffn1_w_in, ffn1_w_out, ffn2_w_in, ffn2_w_out,
              conv_w_in, conv_w, conv_w_out, kv_norm_g, kv_ada_w, kv_ada_b, w_kv,
              attn_w_q, attn_w_o, final_norm_g):
    cond = jax.nn.silu(c)
    cos, sin = rope_tables(positions)
    k_sh = v_sh = None
    for layer in range(DEPTH):
        if layer == N_A_LAYERS:
            kv_shift, kv_scale = jnp.split(cond @ kv_ada_w + kv_ada_b, 2, axis=-1)
            k_sh, v_sh = shared_kv(x, kv_norm_g, kv_shift, kv_scale, w_kv, cos, sin)
        mods = cond @ ada_w[layer] + ada_b[layer]
        sh1, sc1, g1, sh2, sc2, g2, sh3, sc3, g3 = jnp.split(mods, N_MOD, axis=-1)
        h = modulate(rms_norm(x, norm_g[layer, 0]), sh1, sc1)
        x = x + FFN_RES_WEIGHT * (1 + g1)[:, None, :] * swiglu(h, ffn1_w_in[layer], ffn1_w_out[layer])
        h = modulate(rms_norm(x, norm_g[layer, 1]), sh2, sc2)
        if layer < N_A_LAYERS:
            mix = short_conv_mixer(h, conv_w_in[layer], conv_w[layer], conv_w_out[layer])
        else:
            j = layer - N_A_LAYERS
            mix = dilated_attention_mixer(h, attn_w_q[j], attn_w_o[j], k_sh, v_sh, cos, sin)
        x = x + (1 + g2)[:, None, :] * mix
        h = modulate(rms_norm(x, norm_g[layer, 2]), sh3, sc3)
        x = x + FFN_RES_WEIGHT * (1 + g3)[:, None, :] * swiglu(h, ffn2_w_in[layer], ffn2_w_out[layer])
    return rms_norm(x, final_norm_g)
```

```python
import functools

import jax
import jax.numpy as jnp
from jax import lax
from jax.experimental import pallas as pl
from jax.experimental.pallas import tpu as pltpu

F32 = jnp.float32
BF16 = jnp.bfloat16

N_MOD = 9
HEAD_DIM = 64
HEADS_PER_GROUP = 8
DILATED_GROUPS = ((128, 1), (512, 4), (2048, 16))
BAND = 128
ROPE_DIM = HEAD_DIM // 4
ROPE_THETA = 500000.0
NORM_EPS = 1e-5
FFN_RES_WEIGHT = 0.5
GROUP_WIDTH = HEADS_PER_GROUP * HEAD_DIM

LANES = 128
SUBLANES = 8
VMEM_LIMIT_BYTES = 56 * 1024 * 1024

FFN_ROWS = 1024
FFN_CHUNK = 256
CONV_ROWS = 512
CONV_CHUNK = 256
PROJ_ROWS = 512
OPROJ_ROWS = 1024
ROPE_ROWS = 2048
ATTN_ROWS = 2048
ADA_COLS = 1152
MERGE_ROWS = 256


def _params(*sem):
    return pltpu.CompilerParams(dimension_semantics=sem, vmem_limit_bytes=VMEM_LIMIT_BYTES)


def _resident(shape):
    zeros = (0,) * len(shape)
    return pl.BlockSpec(shape, lambda *_: zeros, pipeline_mode=pl.Buffered(1))


def _mod(mods_ref, k, d):
    return mods_ref[0, :, pl.ds(k * d, d)]


def _norm_mod(x, g, shift, scale):
    ms = jnp.mean(x * x, axis=-1, keepdims=True)
    y = x * lax.rsqrt(ms + NORM_EPS)
    return (y * g) * (1.0 + scale) + shift


def _ada_kernel(c_ref, w_ref, b_ref, o_ref):
    c = c_ref[...]
    cond = (c * jax.nn.sigmoid(c)).astype(BF16)
    o_ref[0] = jnp.dot(cond, w_ref[0].astype(BF16), preferred_element_type=F32) + b_ref[0]


def _ada(c8, w, b, cols):
    nl, d, n = w.shape
    return pl.pallas_call(
        _ada_kernel,
        grid=(nl, n // cols),
        in_specs=[
            pl.BlockSpec((SUBLANES, d), lambda l, j: (0, 0)),
            pl.BlockSpec((1, d, cols), lambda l, j: (l, 0, j)),
            pl.BlockSpec((1, 1, cols), lambda l, j: (l, 0, j)),
        ],
        out_specs=pl.BlockSpec((1, SUBLANES, cols), lambda l, j: (l, 0, j)),
        out_shape=jax.ShapeDtypeStruct((nl, SUBLANES, n), F32),
        compiler_params=_params("parallel", "parallel"),
        name="ada",
    )(c8, w, b.reshape(nl, 1, n))


def _rope_kernel(pos_ref, inv_ref, sgn_ref, c_ref, s_ref):
    ang = pos_ref[0].astype(F32) * inv_ref[...]
    c_ref[0] = jnp.cos(ang)
    s_ref[0] = jnp.sin(ang) * sgn_ref[...]


def _rope_tables(positions):
    bsz, seq = positions.shape
    lane = jnp.arange(LANES) % HEAD_DIM
    inv = ROPE_THETA ** (-jnp.arange(0, ROPE_DIM, 2, dtype=F32) / ROPE_DIM)
    rot = lane < ROPE_DIM
    inv_lane = jnp.where(rot, inv[lane % (ROPE_DIM // 2)], 0.0).astype(F32)[None, :]
    sgn_lane = jnp.where(rot, jnp.where(lane < ROPE_DIM // 2, -1.0, 1.0), 0.0).astype(F32)[None, :]
    t = min(ROPE_ROWS, seq)
    tab = jax.ShapeDtypeStruct((bsz, seq, LANES), F32)
    return pl.pallas_call(
        _rope_kernel,
        grid=(bsz, seq // t),
        in_specs=[
            pl.BlockSpec((1, t, 1), lambda b, i: (b, i, 0)),
            pl.BlockSpec((1, LANES), lambda b, i: (0, 0)),
            pl.BlockSpec((1, LANES), lambda b, i: (0, 0)),
        ],
        out_specs=[pl.BlockSpec((1, t, LANES), lambda b, i: (b, i, 0))] * 2,
        out_shape=[tab, tab],
        compiler_params=_params("parallel", "parallel"),
        name="rope",
    )(positions.reshape(bsz, seq, 1), inv_lane, sgn_lane)


def _apply_rope(t, c, s, low):
    partner = jnp.where(low, pltpu.roll(t, LANES - ROPE_DIM // 2, 1), pltpu.roll(t, ROPE_DIM // 2, 1))
    return t * c + partner * s


def _ffn_kernel(*refs, mod_off, n_chunks, final_norm):
    if final_norm:
        x_ref, mods_ref, g_ref, win_ref, wout_ref, fg_ref, o_ref, h_scr, acc_scr = refs
    else:
        x_ref, mods_ref, g_ref, win_ref, wout_ref, o_ref, h_scr, acc_scr = refs
    d = x_ref.shape[-1]
    fc = wout_ref.shape[1]
    h = _norm_mod(x_ref[0], g_ref[...], _mod(mods_ref, mod_off, d), _mod(mods_ref, mod_off + 1, d))
    h_scr[...] = h.astype(BF16)
    acc_scr[...] = jnp.zeros_like(acc_scr)

    def body(c, carry):
        ab = jnp.dot(h_scr[...], win_ref[c], preferred_element_type=F32)
        a = ab[:, :fc]
        act = ((a * jax.nn.sigmoid(a)) * ab[:, fc:]).astype(BF16)
        acc_scr[...] += jnp.dot(act, wout_ref[c], preferred_element_type=F32)
        return carry

    lax.fori_loop(0, n_chunks, body, 0)
    gate = FFN_RES_WEIGHT * (1.0 + _mod(mods_ref, mod_off + 2, d))
    y = x_ref[0] + gate * acc_scr[...]
    if final_norm:
        ms = jnp.mean(y * y, axis=-1, keepdims=True)
        y = (y * lax.rsqrt(ms + NORM_EPS)) * fg_ref[...]
    o_ref[0] = y


def _ffn(x, mods, g, w_in, w_out, mod_off, final_g=None):
    bsz, seq, d = x.shape
    nc, _, fc2 = w_in.shape
    tm = min(FFN_ROWS, seq)
    in_specs = [
        pl.BlockSpec((1, tm, d), lambda b, i: (b, i, 0)),
        pl.BlockSpec((1, 1, mods.shape[-1]), lambda b, i: (b, 0, 0)),
        pl.BlockSpec((1, d), lambda b, i: (0, 0)),
        _resident(w_in.shape),
        _resident(w_out.shape),
    ]
    args = [x, mods, g, w_in, w_out]
    if final_g is not None:
        in_specs.append(pl.BlockSpec((1, d), lambda b, i: (0, 0)))
        args.append(final_g)
    return pl.pallas_call(
        functools.partial(_ffn_kernel, mod_off=mod_off, n_chunks=nc, final_norm=final_g is not None),
        grid=(bsz, seq // tm),
        in_specs=in_specs,
        out_specs=pl.BlockSpec((1, tm, d), lambda b, i: (b, i, 0)),
        out_shape=jax.ShapeDtypeStruct(x.shape, F32),
        scratch_shapes=[pltpu.VMEM((tm, d), BF16), pltpu.VMEM((tm, d), F32)],
        compiler_params=_params("parallel", "parallel"),
        name="ffn",
    )(*args)


def _prep_ffn(w_in, w_out):
    d, f2 = w_in.shape
    f = f2 // 2
    nc = f // FFN_CHUNK
    w = w_in.astype(BF16).reshape(d, 2, nc, FFN_CHUNK)
    w = jnp.transpose(w, (2, 0, 1, 3)).reshape(nc, d, 2 * FFN_CHUNK)
    return w, w_out.astype(BF16).reshape(nc, FFN_CHUNK, d)


def _conv_kernel(x_ref, mods_ref, g_ref, win_ref, cw_ref, wout_ref, o_ref, h_scr, vbuf, carry, acc_scr,
                 *, mod_off, n_chunks):
    tm, d = h_scr.shape
    ck = wout_ref.shape[1]

    @pl.when(pl.program_id(1) == 0)
    def _():
        carry[...] = jnp.zeros_like(carry)

    h = _norm_mod(x_ref[0], g_ref[...], _mod(mods_ref, mod_off, d), _mod(mods_ref, mod_off + 1, d))
    h_scr[...] = h.astype(BF16)
    acc_scr[...] = jnp.zeros_like(acc_scr)
    for j in range(n_chunks):
        z = jnp.dot(h_scr[...], win_ref[j], preferred_element_type=F32)
        v = z[:, ck:2 * ck] * z[:, 2 * ck:]
        vbuf[0:SUBLANES, :] = carry[j]
        vbuf[SUBLANES:, :] = v
        carry[j] = vbuf[tm:, :]
        cw = cw_ref[:, j * ck:(j + 1) * ck]
        conv = (cw[0:1] * vbuf[SUBLANES - 2:tm + SUBLANES - 2, :]
                + cw[1:2] * vbuf[SUBLANES - 1:tm + SUBLANES - 1, :]
                + cw[2:3] * v)
        acc_scr[...] += jnp.dot((z[:, :ck] * conv).astype(BF16), wout_ref[j], preferred_element_type=F32)
    o_ref[0] = x_ref[0] + (1.0 + _mod(mods_ref, mod_off + 2, d)) * acc_scr[...]


def _conv(x, mods, g, w_in, conv_w, w_out, mod_off):
    bsz, seq, d = x.shape
    nc, _, ck3 = w_in.shape
    ck = ck3 // 3
    tm = min(CONV_ROWS, seq)
    return pl.pallas_call(
        functools.partial(_conv_kernel, mod_off=mod_off, n_chunks=nc),
        grid=(bsz, seq // tm),
        in_specs=[
            pl.BlockSpec((1, tm, d), lambda b, i: (b, i, 0)),
            pl.BlockSpec((1, 1, mods.shape[-1]), lambda b, i: (b, 0, 0)),
            pl.BlockSpec((1, d), lambda b, i: (0, 0)),
            _resident(w_in.shape),
            pl.BlockSpec(conv_w.shape, lambda b, i: (0, 0)),
            _resident(w_out.shape),
        ],
        out_specs=pl.BlockSpec((1, tm, d), lambda b, i: (b, i, 0)),
        out_shape=jax.ShapeDtypeStruct(x.shape, F32),
        scratch_shapes=[
            pltpu.VMEM((tm, d), BF16),
            pltpu.VMEM((tm + SUBLANES, ck), F32),
            pltpu.VMEM((nc, SUBLANES, ck), F32),
            pltpu.VMEM((tm, d), F32),
        ],
        compiler_params=_params("arbitrary", "arbitrary"),
        name="conv",
    )(x, mods, g, w_in, conv_w, w_out)


def _prep_conv(w_in, w_out):
    d = w_in.shape[0]
    nc = d // CONV_CHUNK
    w = w_in.astype(BF16).reshape(d, 3, nc, CONV_CHUNK)
    w = jnp.transpose(w, (2, 0, 1, 3)).reshape(nc, d, 3 * CONV_CHUNK)
    return w, w_out.astype(BF16).reshape(nc, CONV_CHUNK, d)


def _proj_kernel(*refs, shift_off, plan):
    x_ref, mods_ref, g_ref, w_ref, c_ref, s_ref = refs[:6]
    outs = refs[6:6 + len(plan)]
    zs = refs[6 + len(plan)]
    tm, d = x_ref.shape[1:]
    h = _norm_mod(x_ref[0], g_ref[...], _mod(mods_ref, shift_off, d), _mod(mods_ref, shift_off + 1, d))
    h = h.astype(BF16)
    low = lax.broadcasted_iota(jnp.int32, (tm, LANES), 1) % HEAD_DIM < ROPE_DIM // 2
    for j, (dil, rope, mult) in enumerate(plan):
        z = jnp.dot(h, w_ref[:, j * GROUP_WIDTH:(j + 1) * GROUP_WIDTH], preferred_element_type=F32)
        for k in range(GROUP_WIDTH // LANES):
            lanes = slice(k * LANES, (k + 1) * LANES)
            zk = z[:, lanes]
            if rope:
                zk = _apply_rope(zk, c_ref[0], s_ref[0], low)
            if mult != 1.0:
                zk = zk * mult
            if dil == 1:
                outs[j][0, 0, :, lanes] = zk.astype(BF16)
            else:
                zs[k] = zk
                for r in range(dil):
                    outs[j][0, r, :, lanes] = zs[k, pl.ds(r, tm // dil, stride=dil), :].astype(BF16)


def _proj(x, mods, g, w, ctab, stab, shift_off, plan):
    bsz, seq, d = x.shape
    tm = min(PROJ_ROWS, seq)
    out_shape = [jax.ShapeDtypeStruct((bsz, dil, seq // dil, GROUP_WIDTH), BF16) for dil, _, _ in plan]
    out_specs = [pl.BlockSpec((1, dil, tm // dil, GROUP_WIDTH), lambda b, i: (b, 0, i, 0)) for dil, _, _ in plan]
    return pl.pallas_call(
        functools.partial(_proj_kernel, shift_off=shift_off, plan=plan),
        grid=(bsz, seq // tm),
        in_specs=[
            pl.BlockSpec((1, tm, d), lambda b, i: (b, i, 0)),
            pl.BlockSpec((1, 1, mods.shape[-1]), lambda b, i: (b, 0, 0)),
            pl.BlockSpec((1, d), lambda b, i: (0, 0)),
            _resident(w.shape),
            pl.BlockSpec((1, tm, LANES), lambda b, i: (b, i, 0)),
            pl.BlockSpec((1, tm, LANES), lambda b, i: (b, i, 0)),
        ],
        out_specs=out_specs,
        out_shape=out_shape,
        scratch_shapes=[pltpu.VMEM((GROUP_WIDTH // LANES, tm, LANES), F32)],
        compiler_params=_params("parallel", "parallel"),
        name="proj",
    )(x, mods, g, w, ctab, stab)


def _attn_kernel(*refs, dils, rows):
    ng = len(dils)
    ins = refs[:5 * ng]
    o_ref = refs[5 * ng]
    scr = refs[5 * ng + 1:]
    kbs, vbs = scr[0:ng], scr[ng:2 * ng]
    accs, mxs, dens = scr[2 * ng:3 * ng], scr[3 * ng:4 * ng], scr[4 * ng:5 * ng]
    bias_ref = scr[5 * ng]
    tile = pl.program_id(1)

    row = lax.broadcasted_iota(jnp.int32, (2 * BAND, 2 * BAND), 0) % BAND
    col = lax.broadcasted_iota(jnp.int32, (2 * BAND, 2 * BAND), 1)
    valid = (col >= row) & (col <= row + BAND)
    bias_ref[0] = jnp.where(valid, 0.0, -jnp.inf)
    bias_ref[1] = jnp.where(valid & (col >= BAND), 0.0, -jnp.inf)
    lo = lax.broadcasted_iota(jnp.int32, (BAND, LANES), 1) < HEAD_DIM

    for g in range(ng):
        dil = dils[g]
        nblk = rows // dil // BAND
        q_ref, k_ref, kh_ref, v_ref, vh_ref = ins[5 * g:5 * g + 5]
        kb, vb = kbs[g], vbs[g]
        kb[:, 0:BAND, :] = kh_ref[0]
        kb[:, BAND:, :] = k_ref[0]
        vb[:, 0:BAND, :] = vh_ref[0]
        vb[:, BAND:, :] = v_ref[0]

        def body(u, carry, dil=dil, nblk=nblk, q_ref=q_ref, kb=kb, vb=vb, g=g):
            r = u // nblk
            j = u % nblk
            row0 = pl.multiple_of(j * BAND, BAND)
            q = q_ref[0, r, pl.ds(row0, BAND), :]
            zero = jnp.zeros_like(q)
            q2 = jnp.concatenate([jnp.where(lo, q, zero), jnp.where(lo, zero, q)], axis=0)
            k = kb[r, pl.ds(row0, 2 * BAND), :]
            v = vb[r, pl.ds(row0, 2 * BAND), :]
            first = jnp.logical_and(tile == 0, j == 0).astype(jnp.int32)
            s = lax.dot_general(q2, k, (((1,), (1,)), ((), ())), preferred_element_type=F32) + bias_ref[first]
            m = jnp.max(s, axis=-1, keepdims=True)
            p = jnp.exp(s - m)
            den = jnp.sum(p, axis=-1, keepdims=True)
            pv = jnp.dot(p.astype(BF16), v, preferred_element_type=F32)
            start = row0 * dil + r
            idx = pl.ds(start, BAND, stride=dil) if dil > 1 else pl.ds(start, BAND)
            accs[g][idx, :] = jnp.where(lo, pv[:BAND], pv[BAND:])
            mxs[g][idx, :] = jnp.where(lo, m[:BAND], m[BAND:])
            dens[g][idx, :] = jnp.where(lo, den[:BAND], den[BAND:])
            return carry

        lax.fori_loop(0, dil * nblk, body, 0)

    def merge(t, carry):
        sl = pl.ds(pl.multiple_of(t * MERGE_ROWS, MERGE_ROWS), MERGE_ROWS)
        ms = [mxs[g][sl, :] for g in range(ng)]
        m_all = functools.reduce(jnp.maximum, ms)
        num = jnp.zeros((MERGE_ROWS, LANES), F32)
        den = jnp.zeros((MERGE_ROWS, LANES), F32)
        for g in range(ng):
            a = jnp.exp(ms[g] - m_all)
            num = num + a * accs[g][sl, :]
            den = den + a * dens[g][sl, :]
        o_ref[0, sl, :] = (num / den).astype(BF16)
        return carry

    lax.fori_loop(0, rows // MERGE_ROWS, merge, 0)


def _attn(qs, ks, vs):
    bsz = qs[0].shape[0]
    dils = tuple(q.shape[1] for q in qs)
    seq = qs[0].shape[1] * qs[0].shape[2]
    rows = min(ATTN_ROWS, seq)
    n_pairs = GROUP_WIDTH // LANES
    in_specs, args, kv_scratch = [], [], []
    for q, k, v in zip(qs, ks, vs):
        dil = q.shape[1]
        blk = rows // dil
        per = blk // BAND
        main = pl.BlockSpec((1, dil, blk, LANES), lambda b, i, hp: (b, 0, i, hp))
        halo = pl.BlockSpec((1, dil, BAND, LANES),
                            lambda b, i, hp, per=per: (b, 0, jnp.maximum(i * per - 1, 0), hp))
        in_specs += [main, main, halo, main, halo]
        args += [q, k, k, v, v]
        kv_scratch.append(pltpu.VMEM((dil, blk + BAND, LANES), BF16))
    stat = [pltpu.VMEM((rows, LANES), F32)] * (3 * len(dils))
    return pl.pallas_call(
        functools.partial(_attn_kernel, dils=dils, rows=rows),
        grid=(bsz, seq // rows, n_pairs),
        in_specs=in_specs,
        out_specs=pl.BlockSpec((1, rows, LANES), lambda b, i, hp: (b, i, hp)),
        out_shape=jax.ShapeDtypeStruct((bsz, seq, GROUP_WIDTH), BF16),
        scratch_shapes=kv_scratch + kv_scratch + stat + [pltpu.VMEM((2, 2 * BAND, 2 * BAND), F32)],
        compiler_params=_params("parallel", "parallel", "parallel"),
        name="attn",
    )(*args)


def _oproj_kernel(x_ref, a_ref, mods_ref, w_ref, o_ref, *, mod_off):
    d = x_ref.shape[-1]
    y = jnp.dot(a_ref[0], w_ref[...], preferred_element_type=F32)
    o_ref[0] = x_ref[0] + (1.0 + _mod(mods_ref, mod_off, d)) * y


def _oproj(x, a, mods, w, mod_off):
    bsz, seq, d = x.shape
    tm = min(OPROJ_ROWS, seq)
    return pl.pallas_call(
        functools.partial(_oproj_kernel, mod_off=mod_off),
        grid=(bsz, seq // tm),
        in_specs=[
            pl.BlockSpec((1, tm, d), lambda b, i: (b, i, 0)),
            pl.BlockSpec((1, tm, a.shape[-1]), lambda b, i: (b, i, 0)),
            pl.BlockSpec((1, 1, mods.shape[-1]), lambda b, i: (b, 0, 0)),
            _resident(w.shape),
        ],
        out_specs=pl.BlockSpec((1, tm, d), lambda b, i: (b, i, 0)),
        out_shape=jax.ShapeDtypeStruct(x.shape, F32),
        compiler_params=_params("parallel", "parallel"),
        name="oproj",
    )(x, a, mods, w)


def kernel(x, c, positions, norm_g, ada_w, ada_b, ffn1_w_in, ffn1_w_out, ffn2_w_in, ffn2_w_out, conv_w_in, conv_w,
           conv_w_out, kv_norm_g, kv_ada_w, kv_ada_b, w_kv, attn_w_q, attn_w_o, final_norm_g):
    bsz, seq, d = x.shape
    depth = ada_w.shape[0]
    n_conv = conv_w_in.shape[0]
    assert all(win // dil == BAND for win, dil in DILATED_GROUPS)
    assert bsz <= SUBLANES and seq % ATTN_ROWS == 0
    dils = tuple(dil for _, dil in DILATED_GROUPS)

    c8 = jnp.zeros((SUBLANES, d), F32).at[:bsz].set(c)
    mods_all = _ada(c8, ada_w, ada_b, ADA_COLS)
    kv_mods = _ada(c8, kv_ada_w[None], kv_ada_b[None], d)[0][:, None, :]
    ctab, stab = _rope_tables(positions)

    ks = vs = None
    for layer in range(depth):
        mods = mods_all[layer][:, None, :]
        g = norm_g[layer]
        if layer == n_conv:
            kv_plan = tuple((dil, True, 1.0) for dil in dils) + tuple((dil, False, 1.0) for dil in dils)
            kv = _proj(x, kv_mods, kv_norm_g[None], w_kv.astype(BF16), ctab, stab, 0, kv_plan)
            ks, vs = kv[:len(dils)], kv[len(dils):]
        w_in, w_out = _prep_ffn(ffn1_w_in[layer], ffn1_w_out[layer])
        x = _ffn(x, mods, g[0:1], w_in, w_out, 0)
        if layer < n_conv:
            w_in, w_out = _prep_conv(conv_w_in[layer], conv_w_out[layer])
            x = _conv(x, mods, g[1:2], w_in, conv_w[layer], w_out, 3)
        else:
            j = layer - n_conv
            q_plan = tuple((dil, True, HEAD_DIM ** -0.5) for dil in dils)
            qs = _proj(x, mods, g[1:2], attn_w_q[j].astype(BF16), ctab, stab, 3, q_plan)
            a = _attn(qs, ks, vs)
            x = _oproj(x, a, mods, attn_w_o[j].astype(BF16), 5)
        w_in, w_out = _prep_ffn(ffn2_w_in[layer], ffn2_w_out[layer])
        x = _ffn(x, mods, g[2:3], w_in, w_out, 6, final_norm_g[None] if layer == depth - 1 else None)
    return x
```

```python
import functools

import jax
import jax.numpy as jnp
from jax import lax
from jax.experimental import pallas as pl
from jax.experimental.pallas import tpu as pltpu

F32 = jnp.float32
BF16 = jnp.bfloat16

N_MOD = 9
HEAD_DIM = 64
HEADS_PER_GROUP = 8
DILATED_GROUPS = ((128, 1), (512, 4), (2048, 16))
BAND = 128
ROPE_DIM = HEAD_DIM // 4
ROPE_THETA = 500000.0
NORM_EPS = 1e-5
FFN_RES_WEIGHT = 0.5
GROUP_WIDTH = HEADS_PER_GROUP * HEAD_DIM

LANES = 128
SUBLANES = 8
VMEM_LIMIT_BYTES = 56 * 1024 * 1024

FFN_ROWS = 1024
FFN_CHUNK = 256
CONV_ROWS = 512
CONV_CHUNK = 256
PROJ_ROWS = 512
OPROJ_ROWS = 1024
ROPE_ROWS = 2048
ATTN_ROWS = 2048
ADA_COLS = 1152
MERGE_ROWS = 256
ATTN_UNROLL = 8


def _params(*sem):
    return pltpu.CompilerParams(dimension_semantics=sem, vmem_limit_bytes=VMEM_LIMIT_BYTES)


def _resident(shape):
    zeros = (0,) * len(shape)
    return pl.BlockSpec(shape, lambda *_: zeros, pipeline_mode=pl.Buffered(1))


def _mod(mods_ref, k, d):
    return mods_ref[0, :, pl.ds(k * d, d)]


def _norm_mod(x, g, shift, scale):
    ms = jnp.mean(x * x, axis=-1, keepdims=True)
    y = x * lax.rsqrt(ms + NORM_EPS)
    return (y * g) * (1.0 + scale) + shift


def _ada_kernel(c_ref, w_ref, b_ref, o_ref):
    c = c_ref[...]
    cond = (c * jax.nn.sigmoid(c)).astype(BF16)
    o_ref[0] = jnp.dot(cond, w_ref[0].astype(BF16), preferred_element_type=F32) + b_ref[0]


def _ada(c8, w, b, cols):
    nl, d, n = w.shape
    return pl.pallas_call(
        _ada_kernel,
        grid=(nl, n // cols),
        in_specs=[
            pl.BlockSpec((SUBLANES, d), lambda l, j: (0, 0)),
            pl.BlockSpec((1, d, cols), lambda l, j: (l, 0, j)),
            pl.BlockSpec((1, 1, cols), lambda l, j: (l, 0, j)),
        ],
        out_specs=pl.BlockSpec((1, SUBLANES, cols), lambda l, j: (l, 0, j)),
        out_shape=jax.ShapeDtypeStruct((nl, SUBLANES, n), F32),
        compiler_params=_params("parallel", "parallel"),
        name="ada",
    )(c8, w, b.reshape(nl, 1, n))


def _rope_kernel(pos_ref, inv_ref, sgn_ref, c_ref, s_ref):
    ang = pos_ref[0].astype(F32) * inv_ref[...]
    c_ref[0] = jnp.cos(ang)
    s_ref[0] = jnp.sin(ang) * sgn_ref[...]


def _rope_tables(positions):
    bsz, seq = positions.shape
    lane = jnp.arange(LANES) % HEAD_DIM
    inv = ROPE_THETA ** (-jnp.arange(0, ROPE_DIM, 2, dtype=F32) / ROPE_DIM)
    rot = lane < ROPE_DIM
    inv_lane = jnp.where(rot, inv[lane % (ROPE_DIM // 2)], 0.0).astype(F32)[None, :]
    sgn_lane = jnp.where(rot, jnp.where(lane < ROPE_DIM // 2, -1.0, 1.0), 0.0).astype(F32)[None, :]
    t = min(ROPE_ROWS, seq)
    tab = jax.ShapeDtypeStruct((bsz, seq, LANES), F32)
    return pl.pallas_call(
        _rope_kernel,
        grid=(bsz, seq // t),
        in_specs=[
            pl.BlockSpec((1, t, 1), lambda b, i: (b, i, 0)),
            pl.BlockSpec((1, LANES), lambda b, i: (0, 0)),
            pl.BlockSpec((1, LANES), lambda b, i: (0, 0)),
        ],
        out_specs=[pl.BlockSpec((1, t, LANES), lambda b, i: (b, i, 0))] * 2,
        out_shape=[tab, tab],
        compiler_params=_params("parallel", "parallel"),
        name="rope",
    )(positions.reshape(bsz, seq, 1), inv_lane, sgn_lane)


def _apply_rope(t, c, s, low):
    partner = jnp.where(low, pltpu.roll(t, LANES - ROPE_DIM // 2, 1), pltpu.roll(t, ROPE_DIM // 2, 1))
    return t * c + partner * s


def _ffn_kernel(*refs, mod_off, n_chunks, final_norm):
    if final_norm:
        x_ref, mods_ref, g_ref, win_ref, wout_ref, fg_ref, o_ref, h_scr, acc_scr = refs
    else:
        x_ref, mods_ref, g_ref, win_ref, wout_ref, o_ref, h_scr, acc_scr = refs
    d = x_ref.shape[-1]
    f = wout_ref.shape[0]
    fc = f // n_chunks
    h = _norm_mod(x_ref[0], g_ref[...], _mod(mods_ref, mod_off, d), _mod(mods_ref, mod_off + 1, d))
    h_scr[...] = h.astype(BF16)
    for c in range(n_chunks):
        a = jnp.dot(h_scr[...], win_ref[:, c * fc:(c + 1) * fc], preferred_element_type=F32)
        b = jnp.dot(h_scr[...], win_ref[:, f + c * fc:f + (c + 1) * fc], preferred_element_type=F32)
        act = ((a * jax.nn.sigmoid(a)) * b).astype(BF16)
        y = jnp.dot(act, wout_ref[c * fc:(c + 1) * fc, :], preferred_element_type=F32)
        if c == 0:
            acc_scr[...] = y
        else:
            acc_scr[...] += y
    gate = FFN_RES_WEIGHT * (1.0 + _mod(mods_ref, mod_off + 2, d))
    y = x_ref[0] + gate * acc_scr[...]
    if final_norm:
        ms = jnp.mean(y * y, axis=-1, keepdims=True)
        y = (y * lax.rsqrt(ms + NORM_EPS)) * fg_ref[...]
    o_ref[0] = y


def _ffn(x, mods, g, w_in, w_out, mod_off, final_g=None):
    bsz, seq, d = x.shape
    nc = w_out.shape[0] // FFN_CHUNK
    tm = min(FFN_ROWS, seq)
    in_specs = [
        pl.BlockSpec((1, tm, d), lambda b, i: (b, i, 0)),
        pl.BlockSpec((1, 1, mods.shape[-1]), lambda b, i: (b, 0, 0)),
        pl.BlockSpec((1, d), lambda b, i: (0, 0)),
        _resident(w_in.shape),
        _resident(w_out.shape),
    ]
    args = [x, mods, g, w_in, w_out]
    if final_g is not None:
        in_specs.append(pl.BlockSpec((1, d), lambda b, i: (0, 0)))
        args.append(final_g)
    return pl.pallas_call(
        functools.partial(_ffn_kernel, mod_off=mod_off, n_chunks=nc, final_norm=final_g is not None),
        grid=(bsz, seq // tm),
        in_specs=in_specs,
        out_specs=pl.BlockSpec((1, tm, d), lambda b, i: (b, i, 0)),
        out_shape=jax.ShapeDtypeStruct(x.shape, F32),
        scratch_shapes=[pltpu.VMEM((tm, d), BF16), pltpu.VMEM((tm, d), F32)],
        compiler_params=_params("parallel", "parallel"),
        name="ffn",
    )(*args)


def _conv_kernel(x_ref, mods_ref, g_ref, win_ref, cw_ref, wout_ref, o_ref, h_scr, vbuf, carry, acc_scr,
                 *, mod_off, n_chunks):
    tm, d = h_scr.shape
    ck = d // n_chunks

    @pl.when(pl.program_id(1) == 0)
    def _():
        carry[...] = jnp.zeros_like(carry)

    h = _norm_mod(x_ref[0], g_ref[...], _mod(mods_ref, mod_off, d), _mod(mods_ref, mod_off + 1, d))
    h_scr[...] = h.astype(BF16)
    for j in range(n_chunks):
        cols = [slice(k * d + j * ck, k * d + (j + 1) * ck) for k in range(3)]
        v = (jnp.dot(h_scr[...], win_ref[:, cols[1]], preferred_element_type=F32)
             * jnp.dot(h_scr[...], win_ref[:, cols[2]], preferred_element_type=F32))
        vbuf[0:SUBLANES, :] = carry[j]
        vbuf[SUBLANES:, :] = v
        carry[j] = vbuf[tm:, :]
        cw = cw_ref[:, j * ck:(j + 1) * ck]
        conv = (cw[0:1] * vbuf[SUBLANES - 2:tm + SUBLANES - 2, :]
                + cw[1:2] * vbuf[SUBLANES - 1:tm + SUBLANES - 1, :]
                + cw[2:3] * v)
        bg = jnp.dot(h_scr[...], win_ref[:, cols[0]], preferred_element_type=F32)
        y = jnp.dot((bg * conv).astype(BF16), wout_ref[j * ck:(j + 1) * ck, :], preferred_element_type=F32)
        if j == 0:
            acc_scr[...] = y
        else:
            acc_scr[...] += y
    o_ref[0] = x_ref[0] + (1.0 + _mod(mods_ref, mod_off + 2, d)) * acc_scr[...]


def _conv(x, mods, g, w_in, conv_w, w_out, mod_off):
    bsz, seq, d = x.shape
    ck = CONV_CHUNK
    nc = d // ck
    tm = min(CONV_ROWS, seq)
    return pl.pallas_call(
        functools.partial(_conv_kernel, mod_off=mod_off, n_chunks=nc),
        grid=(bsz, seq // tm),
        in_specs=[
            pl.BlockSpec((1, tm, d), lambda b, i: (b, i, 0)),
            pl.BlockSpec((1, 1, mods.shape[-1]), lambda b, i: (b, 0, 0)),
            pl.BlockSpec((1, d), lambda b, i: (0, 0)),
            _resident(w_in.shape),
            pl.BlockSpec(conv_w.shape, lambda b, i: (0, 0)),
            _resident(w_out.shape),
        ],
        out_specs=pl.BlockSpec((1, tm, d), lambda b, i: (b, i, 0)),
        out_shape=jax.ShapeDtypeStruct(x.shape, F32),
        scratch_shapes=[
            pltpu.VMEM((tm, d), BF16),
            pltpu.VMEM((tm + SUBLANES, ck), F32),
            pltpu.VMEM((nc, SUBLANES, ck), F32),
            pltpu.VMEM((tm, d), F32),
        ],
        compiler_params=_params("arbitrary", "arbitrary"),
        name="conv",
    )(x, mods, g, w_in, conv_w, w_out)


def _proj_kernel(*refs, shift_off, plan):
    x_ref, mods_ref, g_ref, w_ref, c_ref, s_ref = refs[:6]
    outs = refs[6:6 + len(plan)]
    zs = refs[6 + len(plan)]
    tm, d = x_ref.shape[1:]
    h = _norm_mod(x_ref[0], g_ref[...], _mod(mods_ref, shift_off, d), _mod(mods_ref, shift_off + 1, d))
    h = h.astype(BF16)
    low = lax.broadcasted_iota(jnp.int32, (tm, LANES), 1) % HEAD_DIM < ROPE_DIM // 2
    for j, (dil, rope, mult) in enumerate(plan):
        z = jnp.dot(h, w_ref[:, j * GROUP_WIDTH:(j + 1) * GROUP_WIDTH], preferred_element_type=F32)
        for k in range(GROUP_WIDTH // LANES):
            lanes = slice(k * LANES, (k + 1) * LANES)
            zk = z[:, lanes]
            if rope:
                zk = _apply_rope(zk, c_ref[0], s_ref[0], low)
            if mult != 1.0:
                zk = zk * mult
            if dil == 1:
                outs[j][0, 0, :, lanes] = zk.astype(BF16)
            else:
                zs[k] = zk
                for r in range(dil):
                    outs[j][0, r, :, lanes] = zs[k, pl.ds(r, tm // dil, stride=dil), :].astype(BF16)


def _proj(x, mods, g, w, ctab, stab, shift_off, plan):
    bsz, seq, d = x.shape
    tm = min(PROJ_ROWS, seq)
    out_shape = [jax.ShapeDtypeStruct((bsz, dil, seq // dil, GROUP_WIDTH), BF16) for dil, _, _ in plan]
    out_specs = [pl.BlockSpec((1, dil, tm // dil, GROUP_WIDTH), lambda b, i: (b, 0, i, 0)) for dil, _, _ in plan]
    return pl.pallas_call(
        functools.partial(_proj_kernel, shift_off=shift_off, plan=plan),
        grid=(bsz, seq // tm),
        in_specs=[
            pl.BlockSpec((1, tm, d), lambda b, i: (b, i, 0)),
            pl.BlockSpec((1, 1, mods.shape[-1]), lambda b, i: (b, 0, 0)),
            pl.BlockSpec((1, d), lambda b, i: (0, 0)),
            _resident(w.shape),
            pl.BlockSpec((1, tm, LANES), lambda b, i: (b, i, 0)),
            pl.BlockSpec((1, tm, LANES), lambda b, i: (b, i, 0)),
        ],
        out_specs=out_specs,
        out_shape=out_shape,
        scratch_shapes=[pltpu.VMEM((GROUP_WIDTH // LANES, tm, LANES), F32)],
        compiler_params=_params("parallel", "parallel"),
        name="proj",
    )(x, mods, g, w, ctab, stab)


def _attn_kernel(*refs, dils, rows):
    ng = len(dils)
    ins = refs[:5 * ng]
    o_ref = refs[5 * ng]
    scr = refs[5 * ng + 1:]
    kbs, vbs = scr[0:ng], scr[ng:2 * ng]
    accs, mxs, dens = scr[2 * ng:3 * ng], scr[3 * ng:4 * ng], scr[4 * ng:5 * ng]
    bias_ref = scr[5 * ng]
    tile = pl.program_id(1)

    row = lax.broadcasted_iota(jnp.int32, (2 * BAND, 2 * BAND), 0) % BAND
    col = lax.broadcasted_iota(jnp.int32, (2 * BAND, 2 * BAND), 1)
    valid = (col >= row) & (col <= row + BAND)
    bias_ref[0] = jnp.where(valid, 0.0, -jnp.inf)
    bias_ref[1] = jnp.where(valid & (col >= BAND), 0.0, -jnp.inf)
    lo = lax.broadcasted_iota(jnp.int32, (BAND, LANES), 1) < HEAD_DIM

    for g in range(ng):
        dil = dils[g]
        nblk = rows // dil // BAND
        q_ref, k_ref, kh_ref, v_ref, vh_ref = ins[5 * g:5 * g + 5]
        kb, vb = kbs[g], vbs[g]
        kb[:, 0:BAND, :] = kh_ref[0]
        kb[:, BAND:, :] = k_ref[0]
        vb[:, 0:BAND, :LANES] = vh_ref[0]
        vb[:, BAND:, :LANES] = v_ref[0]
        vb[:, :, LANES:] = jnp.ones((dil, rows // dil + BAND, LANES), BF16)

        def body(u, carry, dil=dil, nblk=nblk, q_ref=q_ref, kb=kb, vb=vb, g=g):
            r = u // nblk
            j = u % nblk
            row0 = pl.multiple_of(j * BAND, BAND)
            q = q_ref[0, r, pl.ds(row0, BAND), :]
            zero = jnp.zeros_like(q)
            q2 = jnp.concatenate([jnp.where(lo, q, zero), jnp.where(lo, zero, q)], axis=0)
            k = kb[r, pl.ds(row0, 2 * BAND), :]
            v = vb[r, pl.ds(row0, 2 * BAND), :]
            first = jnp.logical_and(tile == 0, j == 0).astype(jnp.int32)
            s = lax.dot_general(q2, k, (((1,), (1,)), ((), ())), preferred_element_type=F32) + bias_ref[first]
            m = jnp.max(s, axis=-1, keepdims=True)
            p = jnp.exp(s - m).astype(BF16)
            pv = jnp.dot(p, v, preferred_element_type=F32)
            start = row0 * dil + r
            idx = pl.ds(start, BAND, stride=dil) if dil > 1 else pl.ds(start, BAND)
            accs[g][idx, :] = jnp.where(lo, pv[:BAND, :LANES], pv[BAND:, :LANES])
            mxs[g][idx, :] = jnp.where(lo, m[:BAND], m[BAND:])
            dens[g][idx, :] = jnp.where(lo, pv[:BAND, LANES:], pv[BAND:, LANES:])
            return carry

        lax.fori_loop(0, dil * nblk, body, 0, unroll=ATTN_UNROLL)

    def merge(t, carry):
        sl = pl.ds(pl.multiple_of(t * MERGE_ROWS, MERGE_ROWS), MERGE_ROWS)
        ms = [mxs[g][sl, :] for g in range(ng)]
        m_all = functools.reduce(jnp.maximum, ms)
        num = jnp.zeros((MERGE_ROWS, LANES), F32)
        den = jnp.zeros((MERGE_ROWS, LANES), F32)
        for g in range(ng):
            a = jnp.exp(ms[g] - m_all)
            num = num + a * accs[g][sl, :]
            den = den + a * dens[g][sl, :]
        o_ref[0, sl, :] = (num / den).astype(BF16)
        return carry

    lax.fori_loop(0, rows // MERGE_ROWS, merge, 0)


def _attn(qs, ks, vs):
    bsz = qs[0].shape[0]
    dils = tuple(q.shape[1] for q in qs)
    seq = qs[0].shape[1] * qs[0].shape[2]
    rows = min(ATTN_ROWS, seq)
    n_pairs = GROUP_WIDTH // LANES
    in_specs, args, k_scratch, v_scratch = [], [], [], []
    for q, k, v in zip(qs, ks, vs):
        dil = q.shape[1]
        blk = rows // dil
        per = blk // BAND
        main = pl.BlockSpec((1, dil, blk, LANES), lambda b, i, hp: (b, 0, i, hp))
        halo = pl.BlockSpec((1, dil, BAND, LANES),
                            lambda b, i, hp, per=per: (b, 0, jnp.maximum(i * per - 1, 0), hp))
        in_specs += [main, main, halo, main, halo]
        args += [q, k, k, v, v]
        k_scratch.append(pltpu.VMEM((dil, blk + BAND, LANES), BF16))
        v_scratch.append(pltpu.VMEM((dil, blk + BAND, 2 * LANES), BF16))
    stat = [pltpu.VMEM((rows, LANES), F32)] * (3 * len(dils))
    return pl.pallas_call(
        functools.partial(_attn_kernel, dils=dils, rows=rows),
        grid=(bsz, seq // rows, n_pairs),
        in_specs=in_specs,
        out_specs=pl.BlockSpec((1, rows, LANES), lambda b, i, hp: (b, i, hp)),
        out_shape=jax.ShapeDtypeStruct((bsz, seq, GROUP_WIDTH), BF16),
        scratch_shapes=k_scratch + v_scratch + stat + [pltpu.VMEM((2, 2 * BAND, 2 * BAND), F32)],
        compiler_params=_params("parallel", "parallel", "parallel"),
        name="attn",
    )(*args)


def _oproj_kernel(x_ref, a_ref, mods_ref, w_ref, o_ref, *, mod_off):
    d = x_ref.shape[-1]
    y = jnp.dot(a_ref[0], w_ref[...], preferred_element_type=F32)
    o_ref[0] = x_ref[0] + (1.0 + _mod(mods_ref, mod_off, d)) * y


def _oproj(x, a, mods, w, mod_off):
    bsz, seq, d = x.shape
    tm = min(OPROJ_ROWS, seq)
    return pl.pallas_call(
        functools.partial(_oproj_kernel, mod_off=mod_off),
        grid=(bsz, seq // tm),
        in_specs=[
            pl.BlockSpec((1, tm, d), lambda b, i: (b, i, 0)),
            pl.BlockSpec((1, tm, a.shape[-1]), lambda b, i: (b, i, 0)),
            pl.BlockSpec((1, 1, mods.shape[-1]), lambda b, i: (b, 0, 0)),
            _resident(w.shape),
        ],
        out_specs=pl.BlockSpec((1, tm, d), lambda b, i: (b, i, 0)),
        out_shape=jax.ShapeDtypeStruct(x.shape, F32),
        compiler_params=_params("parallel", "parallel"),
        name="oproj",
    )(x, a, mods, w)


def kernel(x, c, positions, norm_g, ada_w, ada_b, ffn1_w_in, ffn1_w_out, ffn2_w_in, ffn2_w_out, conv_w_in, conv_w,
           conv_w_out, kv_norm_g, kv_ada_w, kv_ada_b, w_kv, attn_w_q, attn_w_o, final_norm_g):
    bsz, seq, d = x.shape
    depth = ada_w.shape[0]
    n_conv = conv_w_in.shape[0]
    assert all(win // dil == BAND for win, dil in DILATED_GROUPS)
    assert bsz <= SUBLANES and seq % ATTN_ROWS == 0
    dils = tuple(dil for _, dil in DILATED_GROUPS)

    c8 = jnp.zeros((SUBLANES, d), F32).at[:bsz].set(c)
    mods_all = _ada(c8, ada_w, ada_b, ADA_COLS)
    kv_mods = _ada(c8, kv_ada_w[None], kv_ada_b[None], d)[0][:, None, :]
    ctab, stab = _rope_tables(positions)

    ks = vs = None
    for layer in range(depth):
        mods = mods_all[layer][:, None, :]
        g = norm_g[layer]
        if layer == n_conv:
            kv_plan = tuple((dil, True, 1.0) for dil in dils) + tuple((dil, False, 1.0) for dil in dils)
            kv = _proj(x, kv_mods, kv_norm_g[None], w_kv.astype(BF16), ctab, stab, 0, kv_plan)
            ks, vs = kv[:len(dils)], kv[len(dils):]
        x = _ffn(x, mods, g[0:1], ffn1_w_in[layer].astype(BF16), ffn1_w_out[layer].astype(BF16), 0)
        if layer < n_conv:
            x = _conv(x, mods, g[1:2], conv_w_in[layer].astype(BF16), conv_w[layer],
                      conv_w_out[layer].astype(BF16), 3)
        else:
            j = layer - n_conv
            q_plan = tuple((dil, True, HEAD_DIM ** -0.5) for dil in dils)
            qs = _proj(x, mods, g[1:2], attn_w_q[j].astype(BF16), ctab, stab, 3, q_plan)
            a = _attn(qs, ks, vs)
            x = _oproj(x, a, mods, attn_w_o[j].astype(BF16), 5)
        x = _ffn(x, mods, g[2:3], ffn2_w_in[layer].astype(BF16), ffn2_w_out[layer].astype(BF16), 6,
                 final_norm_g[None] if layer == depth - 1 else None)
    return x
```

```python
import functools

import jax
import jax.numpy as jnp
from jax import lax
from jax.experimental import pallas as pl
from jax.experimental.pallas import tpu as pltpu

F32 = jnp.float32
BF16 = jnp.bfloat16

N_MOD = 9
HEAD_DIM = 64
HEADS_PER_GROUP = 8
DILATED_GROUPS = ((128, 1), (512, 4), (2048, 16))
BAND = 128
ROPE_DIM = HEAD_DIM // 4
ROPE_THETA = 500000.0
NORM_EPS = 1e-5
FFN_RES_WEIGHT = 0.5
GROUP_WIDTH = HEADS_PER_GROUP * HEAD_DIM

LANES = 128
SUBLANES = 8
VMEM_LIMIT_BYTES = 56 * 1024 * 1024

FFN_ROWS = 1024
FFN_CHUNK = 256
CONV_ROWS = 1024
CONV_CHUNK = 256
PROJ_ROWS = 1024
PROJ_DOT_COLS = 256
OPROJ_ROWS = 1024
ROPE_ROWS = 2048
ATTN_ROWS = 2048
ADA_COLS = 1152
MERGE_ROWS = 256


def _params(*sem):
    return pltpu.CompilerParams(dimension_semantics=sem, vmem_limit_bytes=VMEM_LIMIT_BYTES)


def _resident(shape):
    zeros = (0,) * len(shape)
    return pl.BlockSpec(shape, lambda *_: zeros, pipeline_mode=pl.Buffered(1))


def _mod(mods_ref, k, d):
    return mods_ref[0, :, pl.ds(k * d, d)]


def _norm_mod(x, g, shift, scale):
    ms = jnp.mean(x * x, axis=-1, keepdims=True)
    y = x * lax.rsqrt(ms + NORM_EPS)
    return (y * g) * (1.0 + scale) + shift


def _ada_kernel(c_ref, w_ref, b_ref, o_ref):
    c = c_ref[...]
    cond = (c * jax.nn.sigmoid(c)).astype(BF16)
    o_ref[0] = jnp.dot(cond, w_ref[0].astype(BF16), preferred_element_type=F32) + b_ref[0]


def _ada(c8, w, b, cols):
    nl, d, n = w.shape
    return pl.pallas_call(
        _ada_kernel,
        grid=(nl, n // cols),
        in_specs=[
            pl.BlockSpec((SUBLANES, d), lambda l, j: (0, 0)),
            pl.BlockSpec((1, d, cols), lambda l, j: (l, 0, j)),
            pl.BlockSpec((1, 1, cols), lambda l, j: (l, 0, j)),
        ],
        out_specs=pl.BlockSpec((1, SUBLANES, cols), lambda l, j: (l, 0, j)),
        out_shape=jax.ShapeDtypeStruct((nl, SUBLANES, n), F32),
        compiler_params=_params("parallel", "parallel"),
        name="ada",
    )(c8, w, b.reshape(nl, 1, n))


def _rope_kernel(pos_ref, inv_ref, sgn_ref, c_ref, s_ref):
    ang = pos_ref[0].astype(F32) * inv_ref[...]
    c_ref[0] = jnp.cos(ang)
    s_ref[0] = jnp.sin(ang) * sgn_ref[...]


_HALF_ROT = ROPE_DIM // 2
PAIR_LANE_PERM = (tuple(range(0, _HALF_ROT)) + tuple(range(HEAD_DIM, HEAD_DIM + _HALF_ROT))
                  + tuple(range(ROPE_DIM, HEAD_DIM)) + tuple(range(_HALF_ROT, ROPE_DIM))
                  + tuple(range(HEAD_DIM + _HALF_ROT, HEAD_DIM + ROPE_DIM)) + tuple(range(HEAD_DIM + ROPE_DIM, LANES)))


def _first_head_lanes(lane):
    return (lane < _HALF_ROT) | ((lane >= ROPE_DIM) & (lane < HEAD_DIM + _HALF_ROT))


def _permute_pair_columns(w):
    n = w.shape[-1]
    cols = (jnp.arange(n // LANES)[:, None] * LANES + jnp.asarray(PAIR_LANE_PERM)[None, :]).reshape(n)
    return w[:, cols]


def _rope_tables(positions):
    bsz, seq = positions.shape
    lane = jnp.arange(LANES)
    inv = ROPE_THETA ** (-jnp.arange(0, ROPE_DIM, 2, dtype=F32) / ROPE_DIM)
    first = lane < ROPE_DIM
    second = (lane >= LANES // 2) & (lane < LANES // 2 + ROPE_DIM)
    inv_lane = jnp.where(first | second, inv[lane % _HALF_ROT], 0.0).astype(F32)[None, :]
    sgn_lane = jnp.where(first, -1.0, jnp.where(second, 1.0, 0.0)).astype(F32)[None, :]
    t = min(ROPE_ROWS, seq)
    tab = jax.ShapeDtypeStruct((bsz, seq, LANES), F32)
    return pl.pallas_call(
        _rope_kernel,
        grid=(bsz, seq // t),
        in_specs=[
            pl.BlockSpec((1, t, 1), lambda b, i: (b, i, 0)),
            pl.BlockSpec((1, LANES), lambda b, i: (0, 0)),
            pl.BlockSpec((1, LANES), lambda b, i: (0, 0)),
        ],
        out_specs=[pl.BlockSpec((1, t, LANES), lambda b, i: (b, i, 0))] * 2,
        out_shape=[tab, tab],
        compiler_params=_params("parallel", "parallel"),
        name="rope",
    )(positions.reshape(bsz, seq, 1), inv_lane, sgn_lane)


def _apply_rope(t, c, s):
    return t * c + pltpu.roll(t, LANES // 2, 1) * s


def _ffn_kernel(*refs, mod_off, n_chunks, final_norm):
    if final_norm:
        x_ref, mods_ref, g_ref, win_ref, wout_ref, fg_ref, o_ref, h_scr, acc_scr = refs
    else:
        x_ref, mods_ref, g_ref, win_ref, wout_ref, o_ref, h_scr, acc_scr = refs
    d = x_ref.shape[-1]
    f = wout_ref.shape[0]
    fc = f // n_chunks
    h = _norm_mod(x_ref[0], g_ref[...], _mod(mods_ref, mod_off, d), _mod(mods_ref, mod_off + 1, d))
    h_scr[...] = h.astype(BF16)
    for c in range(n_chunks):
        a = jnp.dot(h_scr[...], win_ref[:, c * fc:(c + 1) * fc], preferred_element_type=F32)
        b = jnp.dot(h_scr[...], win_ref[:, f + c * fc:f + (c + 1) * fc], preferred_element_type=F32)
        act = ((a * jax.nn.sigmoid(a)) * b).astype(BF16)
        y = jnp.dot(act, wout_ref[c * fc:(c + 1) * fc, :], preferred_element_type=F32)
        if c == 0:
            acc_scr[...] = y
        else:
            acc_scr[...] += y
    gate = FFN_RES_WEIGHT * (1.0 + _mod(mods_ref, mod_off + 2, d))
    y = x_ref[0] + gate * acc_scr[...]
    if final_norm:
        ms = jnp.mean(y * y, axis=-1, keepdims=True)
        y = (y * lax.rsqrt(ms + NORM_EPS)) * fg_ref[...]
    o_ref[0] = y


def _ffn(x, mods, g, w_in, w_out, mod_off, final_g=None):
    bsz, seq, d = x.shape
    nc = w_out.shape[0] // FFN_CHUNK
    tm = min(FFN_ROWS, seq)
    in_specs = [
        pl.BlockSpec((1, tm, d), lambda b, i: (b, i, 0)),
        pl.BlockSpec((1, 1, mods.shape[-1]), lambda b, i: (b, 0, 0)),
        pl.BlockSpec((1, d), lambda b, i: (0, 0)),
        _resident(w_in.shape),
        _resident(w_out.shape),
    ]
    args = [x, mods, g, w_in, w_out]
    if final_g is not None:
        in_specs.append(pl.BlockSpec((1, d), lambda b, i: (0, 0)))
        args.append(final_g)
    return pl.pallas_call(
        functools.partial(_ffn_kernel, mod_off=mod_off, n_chunks=nc, final_norm=final_g is not None),
        grid=(bsz, seq // tm),
        in_specs=in_specs,
        out_specs=pl.BlockSpec((1, tm, d), lambda b, i: (b, i, 0)),
        out_shape=jax.ShapeDtypeStruct(x.shape, F32),
        scratch_shapes=[pltpu.VMEM((tm, d), BF16), pltpu.VMEM((tm, d), F32)],
        compiler_params=_params("parallel", "parallel"),
        name="ffn",
    )(*args)


def _conv_kernel(x_ref, mods_ref, g_ref, win_ref, cw_ref, wout_ref, o_ref, h_scr, vbuf, carry, acc_scr,
                 *, mod_off, n_chunks):
    tm, d = h_scr.shape
    ck = d // n_chunks

    @pl.when(pl.program_id(1) == 0)
    def _():
        carry[...] = jnp.zeros_like(carry)

    h = _norm_mod(x_ref[0], g_ref[...], _mod(mods_ref, mod_off, d), _mod(mods_ref, mod_off + 1, d))
    h_scr[...] = h.astype(BF16)
    for j in range(n_chunks):
        cols = [slice(k * d + j * ck, k * d + (j + 1) * ck) for k in range(3)]
        v = (jnp.dot(h_scr[...], win_ref[:, cols[1]], preferred_element_type=F32)
             * jnp.dot(h_scr[...], win_ref[:, cols[2]], preferred_element_type=F32))
        vbuf[0:SUBLANES, :] = carry[j]
        vbuf[SUBLANES:, :] = v
        carry[j] = vbuf[tm:, :]
        cw = cw_ref[:, j * ck:(j + 1) * ck]
        conv = (cw[0:1] * vbuf[SUBLANES - 2:tm + SUBLANES - 2, :]
                + cw[1:2] * vbuf[SUBLANES - 1:tm + SUBLANES - 1, :]
                + cw[2:3] * v)
        bg = jnp.dot(h_scr[...], win_ref[:, cols[0]], preferred_element_type=F32)
        y = jnp.dot((bg * conv).astype(BF16), wout_ref[j * ck:(j + 1) * ck, :], preferred_element_type=F32)
        if j == 0:
            acc_scr[...] = y
        else:
            acc_scr[...] += y
    o_ref[0] = x_ref[0] + (1.0 + _mod(mods_ref, mod_off + 2, d)) * acc_scr[...]


def _conv(x, mods, g, w_in, conv_w, w_out, mod_off):
    bsz, seq, d = x.shape
    ck = CONV_CHUNK
    nc = d // ck
    tm = min(CONV_ROWS, seq)
    return pl.pallas_call(
        functools.partial(_conv_kernel, mod_off=mod_off, n_chunks=nc),
        grid=(bsz, seq // tm),
        in_specs=[
            pl.BlockSpec((1, tm, d), lambda b, i: (b, i, 0)),
            pl.BlockSpec((1, 1, mods.shape[-1]), lambda b, i: (b, 0, 0)),
            pl.BlockSpec((1, d), lambda b, i: (0, 0)),
            _resident(w_in.shape),
            pl.BlockSpec(conv_w.shape, lambda b, i: (0, 0)),
            _resident(w_out.shape),
        ],
        out_specs=pl.BlockSpec((1, tm, d), lambda b, i: (b, i, 0)),
        out_shape=jax.ShapeDtypeStruct(x.shape, F32),
        scratch_shapes=[
            pltpu.VMEM((tm, d), BF16),
            pltpu.VMEM((tm + SUBLANES, ck), F32),
            pltpu.VMEM((nc, SUBLANES, ck), F32),
            pltpu.VMEM((tm, d), F32),
        ],
        compiler_params=_params("arbitrary", "arbitrary"),
        name="conv",
    )(x, mods, g, w_in, conv_w, w_out)


def _proj_kernel(*refs, shift_off, plan):
    x_ref, mods_ref, g_ref, w_ref, c_ref, s_ref = refs[:6]
    outs = refs[6:6 + len(plan)]
    zs, h_scr = refs[6 + len(plan):]
    tm, d = x_ref.shape[1:]
    h = _norm_mod(x_ref[0], g_ref[...], _mod(mods_ref, shift_off, d), _mod(mods_ref, shift_off + 1, d))
    h_scr[...] = h.astype(BF16)
    slices_per_dot = PROJ_DOT_COLS // LANES
    for j in sorted(range(len(plan)), key=lambda n: plan[n][:2], reverse=True):
        dil, rope, mult = plan[j]
        for half in range(GROUP_WIDTH // PROJ_DOT_COLS):
            col0 = j * GROUP_WIDTH + half * PROJ_DOT_COLS
            z = jnp.dot(h_scr[...], w_ref[:, col0:col0 + PROJ_DOT_COLS], preferred_element_type=F32)
            for k in range(slices_per_dot):
                plane = half * slices_per_dot + k
                lanes = slice(plane * LANES, (plane + 1) * LANES)
                zk = z[:, k * LANES:(k + 1) * LANES]
                if rope:
                    zk = _apply_rope(zk, c_ref[0], s_ref[0])
                if mult != 1.0:
                    zk = zk * mult
                if dil == 1:
                    outs[j][0, 0, :, lanes] = zk.astype(BF16)
                else:
                    zs[plane] = zk
                    for r in range(dil):
                        outs[j][0, r, :, lanes] = zs[plane, pl.ds(r, tm // dil, stride=dil), :].astype(BF16)


def _proj(x, mods, g, w, ctab, stab, shift_off, plan):
    bsz, seq, d = x.shape
    tm = min(PROJ_ROWS, seq)
    out_shape = [jax.ShapeDtypeStruct((bsz, dil, seq // dil, GROUP_WIDTH), BF16) for dil, _, _ in plan]
    out_specs = [pl.BlockSpec((1, dil, tm // dil, GROUP_WIDTH), lambda b, i: (b, 0, i, 0)) for dil, _, _ in plan]
    return pl.pallas_call(
        functools.partial(_proj_kernel, shift_off=shift_off, plan=plan),
        grid=(bsz, seq // tm),
        in_specs=[
            pl.BlockSpec((1, tm, d), lambda b, i: (b, i, 0)),
            pl.BlockSpec((1, 1, mods.shape[-1]), lambda b, i: (b, 0, 0)),
            pl.BlockSpec((1, d), lambda b, i: (0, 0)),
            _resident(w.shape),
            pl.BlockSpec((1, tm, LANES), lambda b, i: (b, i, 0)),
            pl.BlockSpec((1, tm, LANES), lambda b, i: (b, i, 0)),
        ],
        out_specs=out_specs,
        out_shape=out_shape,
        scratch_shapes=[pltpu.VMEM((GROUP_WIDTH // LANES, tm, LANES), F32), pltpu.VMEM((tm, d), BF16)],
        compiler_params=_params("parallel", "parallel"),
        name="proj",
    )(x, mods, g, w, ctab, stab)


def _attn_kernel(*refs, dils, rows):
    ng = len(dils)
    ins = refs[:5 * ng]
    o_ref = refs[5 * ng]
    scr = refs[5 * ng + 1:]
    accs, mxs, dens = scr[0:ng], scr[ng:2 * ng], scr[2 * ng:3 * ng]
    bias_ref = scr[3 * ng]
    at_start = (pl.program_id(1) == 0).astype(jnp.int32)

    row = lax.broadcasted_iota(jnp.int32, (2 * BAND, 2 * BAND), 0) % BAND
    col = lax.broadcasted_iota(jnp.int32, (2 * BAND, 2 * BAND), 1)
    valid = (col >= row) & (col <= row + BAND)
    bias_ref[0] = jnp.where(valid, 0.0, -jnp.inf)
    bias_ref[1] = jnp.where(valid & (col >= BAND), 0.0, -jnp.inf)
    lane = lax.broadcasted_iota(jnp.int32, (BAND, LANES), 1)
    q_first = _first_head_lanes(lane)
    lo = lane < HEAD_DIM
    ones = jnp.ones((2 * BAND, LANES), BF16)

    for g in range(ng):
        dil = dils[g]
        q_ref, k_ref, kh_ref, v_ref, vh_ref = ins[5 * g:5 * g + 5]
        for r in range(dil):
            for j in range(rows // dil // BAND):
                cur = slice(j * BAND, (j + 1) * BAND)
                q = q_ref[0, r, cur, :]
                zero = jnp.zeros_like(q)
                q2 = jnp.concatenate([jnp.where(q_first, q, zero), jnp.where(q_first, zero, q)], axis=0)
                if j == 0:
                    k = jnp.concatenate([kh_ref[0, r], k_ref[0, r, cur, :]], axis=0)
                    v = jnp.concatenate([vh_ref[0, r], v_ref[0, r, cur, :]], axis=0)
                    bias = bias_ref[at_start]
                else:
                    both = slice((j - 1) * BAND, (j + 1) * BAND)
                    k = k_ref[0, r, both, :]
                    v = v_ref[0, r, both, :]
                    bias = bias_ref[0]
                s = lax.dot_general(q2, k, (((1,), (1,)), ((), ())), preferred_element_type=F32) + bias
                m = jnp.max(s, axis=-1, keepdims=True)
                p = jnp.exp(s - m).astype(BF16)
                pv = jnp.dot(p, jnp.concatenate([v, ones], axis=1), preferred_element_type=F32)
                start = j * BAND * dil + r
                idx = pl.ds(start, BAND, stride=dil) if dil > 1 else pl.ds(start, BAND)
                accs[g][idx, :] = jnp.where(lo, pv[:BAND, :LANES], pv[BAND:, :LANES])
                mxs[g][idx, :] = jnp.where(lo, m[:BAND], m[BAND:])
                dens[g][idx, :] = jnp.where(lo, pv[:BAND, LANES:], pv[BAND:, LANES:])

    def merge(t, carry):
        sl = pl.ds(pl.multiple_of(t * MERGE_ROWS, MERGE_ROWS), MERGE_ROWS)
        ms = [mxs[g][sl, :] for g in range(ng)]
        m_all = functools.reduce(jnp.maximum, ms)
        num = jnp.zeros((MERGE_ROWS, LANES), F32)
        den = jnp.zeros((MERGE_ROWS, LANES), F32)
        for g in range(ng):
            a = jnp.exp(ms[g] - m_all)
            num = num + a * accs[g][sl, :]
            den = den + a * dens[g][sl, :]
        o_ref[0, sl, :] = (num / den).astype(BF16)
        return carry

    lax.fori_loop(0, rows // MERGE_ROWS, merge, 0)


def _attn(qs, ks, vs):
    bsz = qs[0].shape[0]
    dils = tuple(q.shape[1] for q in qs)
    seq = qs[0].shape[1] * qs[0].shape[2]
    rows = min(ATTN_ROWS, seq)
    n_pairs = GROUP_WIDTH // LANES
    in_specs, args = [], []
    for q, k, v in zip(qs, ks, vs):
        dil = q.shape[1]
        blk = rows // dil
        per = blk // BAND
        main = pl.BlockSpec((1, dil, blk, LANES), lambda b, i, hp: (b, 0, i, hp))
        halo = pl.BlockSpec((1, dil, BAND, LANES),
                            lambda b, i, hp, per=per: (b, 0, jnp.maximum(i * per - 1, 0), hp))
        in_specs += [main, main, halo, main, halo]
        args += [q, k, k, v, v]
    stat = [pltpu.VMEM((rows, LANES), F32)] * (3 * len(dils))
    return pl.pallas_call(
        functools.partial(_attn_kernel, dils=dils, rows=rows),
        grid=(bsz, seq // rows, n_pairs),
        in_specs=in_specs,
        out_specs=pl.BlockSpec((1, rows, LANES), lambda b, i, hp: (b, i, hp)),
        out_shape=jax.ShapeDtypeStruct((bsz, seq, GROUP_WIDTH), BF16),
        scratch_shapes=stat + [pltpu.VMEM((2, 2 * BAND, 2 * BAND), F32)],
        compiler_params=_params("parallel", "parallel", "parallel"),
        name="attn",
    )(*args)


def _oproj_kernel(x_ref, a_ref, mods_ref, w_ref, o_ref, *, mod_off):
    d = x_ref.shape[-1]
    y = jnp.dot(a_ref[0], w_ref[...], preferred_element_type=F32)
    o_ref[0] = x_ref[0] + (1.0 + _mod(mods_ref, mod_off, d)) * y


def _oproj(x, a, mods, w, mod_off):
    bsz, seq, d = x.shape
    tm = min(OPROJ_ROWS, seq)
    return pl.pallas_call(
        functools.partial(_oproj_kernel, mod_off=mod_off),
        grid=(bsz, seq // tm),
        in_specs=[
            pl.BlockSpec((1, tm, d), lambda b, i: (b, i, 0)),
            pl.BlockSpec((1, tm, a.shape[-1]), lambda b, i: (b, i, 0)),
            pl.BlockSpec((1, 1, mods.shape[-1]), lambda b, i: (b, 0, 0)),
            _resident(w.shape),
        ],
        out_specs=pl.BlockSpec((1, tm, d), lambda b, i: (b, i, 0)),
        out_shape=jax.ShapeDtypeStruct(x.shape, F32),
        compiler_params=_params("parallel", "parallel"),
        name="oproj",
    )(x, a, mods, w)


def kernel(x, c, positions, norm_g, ada_w, ada_b, ffn1_w_in, ffn1_w_out, ffn2_w_in, ffn2_w_out, conv_w_in, conv_w,
           conv_w_out, kv_norm_g, kv_ada_w, kv_ada_b, w_kv, attn_w_q, attn_w_o, final_norm_g):
    bsz, seq, d = x.shape
    depth = ada_w.shape[0]
    n_conv = conv_w_in.shape[0]
    assert all(win // dil == BAND for win, dil in DILATED_GROUPS)
    assert bsz <= SUBLANES and seq % ATTN_ROWS == 0
    dils = tuple(dil for _, dil in DILATED_GROUPS)

    c8 = jnp.zeros((SUBLANES, d), F32).at[:bsz].set(c)
    mods_all = _ada(c8, ada_w, ada_b, ADA_COLS)
    kv_mods = _ada(c8, kv_ada_w[None], kv_ada_b[None], d)[0][:, None, :]
    ctab, stab = _rope_tables(positions)

    ks = vs = None
    for layer in range(depth):
        mods = mods_all[layer][:, None, :]
        g = norm_g[layer]
        if layer == n_conv:
            kv_plan = tuple((dil, True, 1.0) for dil in dils) + tuple((dil, False, 1.0) for dil in dils)
            k_width = w_kv.shape[1] // 2
            w_kv_p = jnp.concatenate([_permute_pair_columns(w_kv[:, :k_width]), w_kv[:, k_width:]], axis=1)
            kv = _proj(x, kv_mods, kv_norm_g[None], w_kv_p.astype(BF16), ctab, stab, 0, kv_plan)
            ks, vs = kv[:len(dils)], kv[len(dils):]
        x = _ffn(x, mods, g[0:1], ffn1_w_in[layer].astype(BF16), ffn1_w_out[layer].astype(BF16), 0)
        if layer < n_conv:
            x = _conv(x, mods, g[1:2], conv_w_in[layer].astype(BF16), conv_w[layer],
                      conv_w_out[layer].astype(BF16), 3)
        else:
            j = layer - n_conv
            q_plan = tuple((dil, True, HEAD_DIM ** -0.5) for dil in dils)
            qs = _proj(x, mods, g[1:2], _permute_pair_columns(attn_w_q[j]).astype(BF16), ctab, stab, 3, q_plan)
            a = _attn(qs, ks, vs)
            x = _oproj(x, a, mods, attn_w_o[j].astype(BF16), 5)
        x = _ffn(x, mods, g[2:3], ffn2_w_in[layer].astype(BF16), ffn2_w_out[layer].astype(BF16), 6,
                 final_norm_g[None] if layer == depth - 1 else None)
    return x
```

```python
import functools

import jax
import jax.numpy as jnp
from jax import lax
from jax.experimental import pallas as pl
from jax.experimental.pallas import tpu as pltpu

F32 = jnp.float32
BF16 = jnp.bfloat16

N_MOD = 9
HEAD_DIM = 64
HEADS_PER_GROUP = 8
DILATED_GROUPS = ((128, 1), (512, 4), (2048, 16))
BAND = 128
ROPE_DIM = HEAD_DIM // 4
ROPE_THETA = 500000.0
NORM_EPS = 1e-5
FFN_RES_WEIGHT = 0.5
LOG2_E = 1.4426950408889634
GROUP_WIDTH = HEADS_PER_GROUP * HEAD_DIM

LANES = 128
SUBLANES = 8
VMEM_LIMIT_BYTES = 56 * 1024 * 1024

FFN_ROWS = 1024
FFN_CHUNK = 256
CONV_ROWS = 1024
CONV_CHUNK = 512
PROJ_ROWS = 1024
PROJ_DOT_COLS = 256
ROPE_ROWS = 2048
ATTN_ROWS = 2048
ADA_COLS = 1152


def _params(*sem):
    return pltpu.CompilerParams(dimension_semantics=sem, vmem_limit_bytes=VMEM_LIMIT_BYTES)


def _layer_spec(stacked, layer):
    index = (layer,) + (0,) * (stacked.ndim - 1)
    return pl.BlockSpec((None,) + stacked.shape[1:], lambda *_: index, pipeline_mode=pl.Buffered(1))


def _mod(mods_ref, k, d):
    return mods_ref[pl.ds(pl.program_id(0), 1), pl.ds(k * d, d)]


def _norm_mod(x, g, shift, scale):
    ms = jnp.mean(x * x, axis=-1, keepdims=True)
    y = x * lax.rsqrt(ms + NORM_EPS)
    return (y * g) * (1.0 + scale) + shift


def _ada_kernel(c_ref, w_ref, b_ref, o_ref):
    c = c_ref[...]
    cond = (c * jax.nn.sigmoid(c)).astype(BF16)
    o_ref[0] = jnp.dot(cond, w_ref[0].astype(BF16), preferred_element_type=F32) + b_ref[0]


def _ada(c8, w, b, cols):
    nl, d, n = w.shape
    return pl.pallas_call(
        _ada_kernel,
        grid=(nl, n // cols),
        in_specs=[
            pl.BlockSpec((SUBLANES, d), lambda l, j: (0, 0)),
            pl.BlockSpec((1, d, cols), lambda l, j: (l, 0, j)),
            pl.BlockSpec((1, 1, cols), lambda l, j: (l, 0, j)),
        ],
        out_specs=pl.BlockSpec((1, SUBLANES, cols), lambda l, j: (l, 0, j)),
        out_shape=jax.ShapeDtypeStruct((nl, SUBLANES, n), F32),
        compiler_params=_params("parallel", "parallel"),
        name="ada",
    )(c8, w, b.reshape(nl, 1, n))


def _rope_kernel(pos_ref, inv_ref, sgn_ref, c_ref, s_ref):
    ang = pos_ref[0].astype(F32) * inv_ref[...]
    c_ref[0] = jnp.cos(ang)
    s_ref[0] = jnp.sin(ang) * sgn_ref[...]


_HALF_ROT = ROPE_DIM // 2
PAIR_LANE_PERM = (tuple(range(0, _HALF_ROT)) + tuple(range(HEAD_DIM, HEAD_DIM + _HALF_ROT))
                  + tuple(range(ROPE_DIM, HEAD_DIM)) + tuple(range(_HALF_ROT, ROPE_DIM))
                  + tuple(range(HEAD_DIM + _HALF_ROT, HEAD_DIM + ROPE_DIM)) + tuple(range(HEAD_DIM + ROPE_DIM, LANES)))


def _first_head_lanes(lane):
    return (lane < _HALF_ROT) | ((lane >= ROPE_DIM) & (lane < HEAD_DIM + _HALF_ROT))


def _permute_pair_columns(w):
    n = w.shape[-1]
    cols = (jnp.arange(n // LANES)[:, None] * LANES + jnp.asarray(PAIR_LANE_PERM)[None, :]).reshape(n)
    return w[..., cols]


def _rope_tables(positions):
    bsz, seq = positions.shape
    lane = jnp.arange(LANES)
    inv = ROPE_THETA ** (-jnp.arange(0, ROPE_DIM, 2, dtype=F32) / ROPE_DIM)
    first = lane < ROPE_DIM
    second = (lane >= LANES // 2) & (lane < LANES // 2 + ROPE_DIM)
    inv_lane = jnp.where(first | second, inv[lane % _HALF_ROT], 0.0).astype(F32)[None, :]
    sgn_lane = jnp.where(first, -1.0, jnp.where(second, 1.0, 0.0)).astype(F32)[None, :]
    t = min(ROPE_ROWS, seq)
    tab = jax.ShapeDtypeStruct((bsz, seq, LANES), F32)
    return pl.pallas_call(
        _rope_kernel,
        grid=(bsz, seq // t),
        in_specs=[
            pl.BlockSpec((1, t, 1), lambda b, i: (b, i, 0)),
            pl.BlockSpec((1, LANES), lambda b, i: (0, 0)),
            pl.BlockSpec((1, LANES), lambda b, i: (0, 0)),
        ],
        out_specs=[pl.BlockSpec((1, t, LANES), lambda b, i: (b, i, 0))] * 2,
        out_shape=[tab, tab],
        compiler_params=_params("parallel", "parallel"),
        name="rope",
    )(positions.reshape(bsz, seq, 1), inv_lane, sgn_lane)


def _apply_rope(t, c, s):
    return t * c + pltpu.roll(t, LANES // 2, 1) * s


def _ffn_kernel(*refs, g_row, mod_off, n_chunks, mix_gate_off, final_norm):
    x_ref, mods_ref, g_ref, win_ref, wout_ref = refs[:5]
    rest = list(refs[5:])
    a_ref, wo_ref = (rest.pop(0), rest.pop(0)) if mix_gate_off is not None else (None, None)
    fg_ref = rest.pop(0) if final_norm else None
    o_ref, h_scr = rest
    d = x_ref.shape[-1]
    f = wout_ref.shape[0]
    fc = f // n_chunks
    x = x_ref[0]
    if mix_gate_off is not None:
        mix = jnp.dot(a_ref[0], wo_ref[...], preferred_element_type=F32)
        x = x + (1.0 + _mod(mods_ref, mix_gate_off, d)) * mix
    o_ref[0] = x
    h = _norm_mod(x, g_ref[g_row:g_row + 1, :], _mod(mods_ref, mod_off, d), _mod(mods_ref, mod_off + 1, d))
    h_scr[...] = h.astype(BF16)
    gate = FFN_RES_WEIGHT * (1.0 + _mod(mods_ref, mod_off + 2, d))
    for c in range(n_chunks):
        a = jnp.dot(h_scr[...], win_ref[:, c * fc:(c + 1) * fc], preferred_element_type=F32)
        b = jnp.dot(h_scr[...], win_ref[:, f + c * fc:f + (c + 1) * fc], preferred_element_type=F32)
        act = ((a * jax.nn.sigmoid(a)) * b).astype(BF16)
        o_ref[0] += gate * jnp.dot(act, wout_ref[c * fc:(c + 1) * fc, :], preferred_element_type=F32)
    if final_norm:
        y = o_ref[0]
        ms = jnp.mean(y * y, axis=-1, keepdims=True)
        o_ref[0] = (y * lax.rsqrt(ms + NORM_EPS)) * fg_ref[...]


def _ffn(x, mods, norm_g, w_in, w_out, layer, g_row, mod_off, mix=None, final_g=None):
    bsz, seq, d = x.shape
    nc = w_out.shape[1] // FFN_CHUNK
    tm = min(FFN_ROWS, seq)
    in_specs = [
        pl.BlockSpec((1, tm, d), lambda b, i: (b, i, 0)),
        _layer_spec(mods, layer),
        _layer_spec(norm_g, layer),
        _layer_spec(w_in, layer),
        _layer_spec(w_out, layer),
    ]
    args = [x, mods, norm_g, w_in, w_out]
    mix_gate_off = None
    if mix is not None:
        a, w_o, j, mix_gate_off = mix
        in_specs += [pl.BlockSpec((1, tm, a.shape[-1]), lambda b, i: (b, i, 0)), _layer_spec(w_o, j)]
        args += [a, w_o]
    if final_g is not None:
        in_specs.append(pl.BlockSpec((1, d), lambda b, i: (0, 0)))
        args.append(final_g)
    return pl.pallas_call(
        functools.partial(_ffn_kernel, g_row=g_row, mod_off=mod_off, n_chunks=nc, mix_gate_off=mix_gate_off,
                          final_norm=final_g is not None),
        grid=(bsz, seq // tm),
        in_specs=in_specs,
        out_specs=pl.BlockSpec((1, tm, d), lambda b, i: (b, i, 0)),
        out_shape=jax.ShapeDtypeStruct(x.shape, F32),
        scratch_shapes=[pltpu.VMEM((tm, d), BF16)],
        compiler_params=_params("parallel", "parallel"),
        name="ffn",
    )(*args)


def _conv_kernel(x_ref, mods_ref, g_ref, win_ref, cw_ref, wout_ref, o_ref, h_scr, vbuf, carry,
                 *, g_row, mod_off, n_chunks):
    tm, d = h_scr.shape
    ck = d // n_chunks

    @pl.when(pl.program_id(1) == 0)
    def _():
        carry[...] = jnp.zeros_like(carry)

    h = _norm_mod(x_ref[0], g_ref[g_row:g_row + 1, :], _mod(mods_ref, mod_off, d), _mod(mods_ref, mod_off + 1, d))
    h_scr[...] = h.astype(BF16)
    gate = 1.0 + _mod(mods_ref, mod_off + 2, d)
    for j in range(n_chunks):
        cols = [slice(k * d + j * ck, k * d + (j + 1) * ck) for k in range(3)]
        v = (jnp.dot(h_scr[...], win_ref[:, cols[1]], preferred_element_type=F32)
             * jnp.dot(h_scr[...], win_ref[:, cols[2]], preferred_element_type=F32))
        vbuf[0:SUBLANES, :] = carry[j]
        vbuf[SUBLANES:, :] = v
        carry[j] = vbuf[tm:, :]
        cw = cw_ref[:, j * ck:(j + 1) * ck]
        conv = (cw[0:1] * vbuf[SUBLANES - 2:tm + SUBLANES - 2, :]
                + cw[1:2] * vbuf[SUBLANES - 1:tm + SUBLANES - 1, :]
                + cw[2:3] * v)
        bg = jnp.dot(h_scr[...], win_ref[:, cols[0]], preferred_element_type=F32)
        y = jnp.dot((bg * conv).astype(BF16), wout_ref[j * ck:(j + 1) * ck, :], preferred_element_type=F32)
        if j == 0:
            o_ref[0] = x_ref[0] + gate * y
        else:
            o_ref[0] += gate * y


def _conv(x, mods, norm_g, w_in, conv_w, w_out, layer, g_row, mod_off):
    bsz, seq, d = x.shape
    ck = CONV_CHUNK
    nc = d // ck
    tm = min(CONV_ROWS, seq)
    return pl.pallas_call(
        functools.partial(_conv_kernel, g_row=g_row, mod_off=mod_off, n_chunks=nc),
        grid=(bsz, seq // tm),
        in_specs=[
            pl.BlockSpec((1, tm, d), lambda b, i: (b, i, 0)),
            _layer_spec(mods, layer),
            _layer_spec(norm_g, layer),
            _layer_spec(w_in, layer),
            _layer_spec(conv_w, layer),
            _layer_spec(w_out, layer),
        ],
        out_specs=pl.BlockSpec((1, tm, d), lambda b, i: (b, i, 0)),
        out_shape=jax.ShapeDtypeStruct(x.shape, F32),
        scratch_shapes=[
            pltpu.VMEM((tm, d), BF16),
            pltpu.VMEM((tm + SUBLANES, ck), F32),
            pltpu.VMEM((nc, SUBLANES, ck), F32),
        ],
        compiler_params=_params("arbitrary", "arbitrary"),
        name="conv",
    )(x, mods, norm_g, w_in, conv_w, w_out)


def _proj_kernel(*refs, g_row, shift_off, plan):
    x_ref, mods_ref, g_ref, w_ref, c_ref, s_ref = refs[:6]
    outs = refs[6:6 + len(plan)]
    zs, h_scr = refs[6 + len(plan):]
    tm, d = x_ref.shape[1:]
    h = _norm_mod(x_ref[0], g_ref[g_row:g_row + 1, :], _mod(mods_ref, shift_off, d),
                  _mod(mods_ref, shift_off + 1, d))
    h_scr[...] = h.astype(BF16)
    slices_per_dot = PROJ_DOT_COLS // LANES
    for j in sorted(range(len(plan)), key=lambda n: plan[n][:2], reverse=True):
        dil, rope, mult = plan[j]
        for half in range(GROUP_WIDTH // PROJ_DOT_COLS):
            col0 = j * GROUP_WIDTH + half * PROJ_DOT_COLS
            z = jnp.dot(h_scr[...], w_ref[:, col0:col0 + PROJ_DOT_COLS], preferred_element_type=F32)
            for k in range(slices_per_dot):
                plane = half * slices_per_dot + k
                lanes = slice(plane * LANES, (plane + 1) * LANES)
                zk = z[:, k * LANES:(k + 1) * LANES]
                if rope:
                    zk = _apply_rope(zk, c_ref[0], s_ref[0])
                if mult != 1.0:
                    zk = zk * mult
                if dil == 1:
                    outs[j][0, 0, :, lanes] = zk.astype(BF16)
                else:
                    zs[plane] = zk
                    for r in range(dil):
                        outs[j][0, r, :, lanes] = zs[plane, pl.ds(r, tm // dil, stride=dil), :].astype(BF16)


def _proj(x, mods, norm_g, w, ctab, stab, layer, g_row, w_index, shift_off, plan):
    bsz, seq, d = x.shape
    tm = min(PROJ_ROWS, seq)
    out_shape = [jax.ShapeDtypeStruct((bsz, dil, seq // dil, GROUP_WIDTH), BF16) for dil, _, _ in plan]
    out_specs = [pl.BlockSpec((1, dil, tm // dil, GROUP_WIDTH), lambda b, i: (b, 0, i, 0)) for dil, _, _ in plan]
    return pl.pallas_call(
        functools.partial(_proj_kernel, g_row=g_row, shift_off=shift_off, plan=plan),
        grid=(bsz, seq // tm),
        in_specs=[
            pl.BlockSpec((1, tm, d), lambda b, i: (b, i, 0)),
            _layer_spec(mods, layer),
            _layer_spec(norm_g, layer),
            _layer_spec(w, w_index),
            pl.BlockSpec((1, tm, LANES), lambda b, i: (b, i, 0)),
            pl.BlockSpec((1, tm, LANES), lambda b, i: (b, i, 0)),
        ],
        out_specs=out_specs,
        out_shape=out_shape,
        scratch_shapes=[pltpu.VMEM((GROUP_WIDTH // LANES, tm, LANES), F32), pltpu.VMEM((tm, d), BF16)],
        compiler_params=_params("parallel", "parallel"),
        name="proj",
    )(x, mods, norm_g, w, ctab, stab)


def _attn_kernel(*refs, dils, rows):
    ng = len(dils)
    ins = refs[:5 * ng]
    o_ref = refs[5 * ng]
    scr = refs[5 * ng + 1:]
    *strided, dense = sorted(range(ng), key=lambda g: dils[g], reverse=True)
    assert dils[dense] == 1
    accs = dict(zip(strided, scr[0:ng - 1]))
    mxs = dict(zip(strided, scr[ng - 1:2 * ng - 2]))
    dens = dict(zip(strided, scr[2 * ng - 2:3 * ng - 3]))
    bias_ref = scr[3 * ng - 3]
    at_start = (pl.program_id(1) == 0).astype(jnp.int32)

    row = lax.broadcasted_iota(jnp.int32, (2 * BAND, 2 * BAND), 0) % BAND
    col = lax.broadcasted_iota(jnp.int32, (2 * BAND, 2 * BAND), 1)
    valid = (col >= row) & (col <= row + BAND)
    bias_ref[0] = jnp.where(valid, 0.0, -jnp.inf)
    bias_ref[1] = jnp.where(valid & (col >= BAND), 0.0, -jnp.inf)
    lane = lax.broadcasted_iota(jnp.int32, (BAND, LANES), 1)
    q_first = _first_head_lanes(lane)
    lo = lane < HEAD_DIM
    ones = jnp.ones((2 * BAND, LANES), BF16)

    def block_stats(g, r, j):
        q_ref, k_ref, kh_ref, v_ref, vh_ref = ins[5 * g:5 * g + 5]
        cur = slice(j * BAND, (j + 1) * BAND)
        q = q_ref[0, r, cur, :]
        zero = jnp.zeros_like(q)
        q2 = jnp.concatenate([jnp.where(q_first, q, zero), jnp.where(q_first, zero, q)], axis=0)
        if j == 0:
            k = jnp.concatenate([kh_ref[0, r], k_ref[0, r, cur, :]], axis=0)
            v = jnp.concatenate([vh_ref[0, r], v_ref[0, r, cur, :]], axis=0)
            bias = bias_ref[at_start]
        else:
            both = slice((j - 1) * BAND, (j + 1) * BAND)
            k = k_ref[0, r, both, :]
            v = v_ref[0, r, both, :]
            bias = bias_ref[0]
        s = lax.dot_general(q2, k, (((1,), (1,)), ((), ())), preferred_element_type=F32) + bias
        m = jnp.max(s, axis=-1, keepdims=True)
        p = jnp.exp2(s - m).astype(BF16)
        pv = jnp.dot(p, jnp.concatenate([v, ones], axis=1), preferred_element_type=F32)
        return (jnp.where(lo, pv[:BAND, :LANES], pv[BAND:, :LANES]), jnp.where(lo, m[:BAND], m[BAND:]),
                jnp.where(lo, pv[:BAND, LANES:], pv[BAND:, LANES:]))

    for g in strided:
        dil = dils[g]
        for r in range(dil):
            for j in range(rows // dil // BAND):
                idx = pl.ds(j * BAND * dil + r, BAND, stride=dil)
                accs[g][idx, :], mxs[g][idx, :], dens[g][idx, :] = block_stats(g, r, j)
    for j in range(rows // BAND):
        sl = slice(j * BAND, (j + 1) * BAND)
        parts = [block_stats(dense, 0, j)] + [(accs[g][sl, :], mxs[g][sl, :], dens[g][sl, :]) for g in strided]
        m_all = functools.reduce(jnp.maximum, [m for _, m, _ in parts])
        num = den = None
        for acc, m, l in parts:
            a = jnp.exp2(m - m_all)
            num = a * acc if num is None else num + a * acc
            den = a * l if den is None else den + a * l
        o_ref[0, sl, :] = (num / den).astype(BF16)


def _attn(qs, ks, vs):
    bsz = qs[0].shape[0]
    dils = tuple(q.shape[1] for q in qs)
    seq = qs[0].shape[1] * qs[0].shape[2]
    rows = min(ATTN_ROWS, seq)
    n_pairs = GROUP_WIDTH // LANES
    in_specs, args = [], []
    for q, k, v in zip(qs, ks, vs):
        dil = q.shape[1]
        blk = rows // dil
        per = blk // BAND
        main = pl.BlockSpec((1, dil, blk, LANES), lambda b, i, hp: (b, 0, i, hp))
        halo = pl.BlockSpec((1, dil, BAND, LANES),
                            lambda b, i, hp, per=per: (b, 0, jnp.maximum(i * per - 1, 0), hp))
        in_specs += [main, main, halo, main, halo]
        args += [q, k, k, v, v]
    stat = [pltpu.VMEM((rows, LANES), F32)] * (3 * (len(dils) - 1))
    return pl.pallas_call(
        functools.partial(_attn_kernel, dils=dils, rows=rows),
        grid=(bsz, seq // rows, n_pairs),
        in_specs=in_specs,
        out_specs=pl.BlockSpec((1, rows, LANES), lambda b, i, hp: (b, i, hp)),
        out_shape=jax.ShapeDtypeStruct((bsz, seq, GROUP_WIDTH), BF16),
        scratch_shapes=stat + [pltpu.VMEM((2, 2 * BAND, 2 * BAND), F32)],
        compiler_params=_params("parallel", "parallel", "parallel"),
        name="attn",
    )(*args)


def kernel(x, c, positions, norm_g, ada_w, ada_b, ffn1_w_in, ffn1_w_out, ffn2_w_in, ffn2_w_out, conv_w_in, conv_w,
           conv_w_out, kv_norm_g, kv_ada_w, kv_ada_b, w_kv, attn_w_q, attn_w_o, final_norm_g):
    bsz, seq, d = x.shape
    depth = ada_w.shape[0]
    n_conv = conv_w_in.shape[0]
    assert all(win // dil == BAND for win, dil in DILATED_GROUPS)
    assert bsz <= SUBLANES and seq % ATTN_ROWS == 0
    dils = tuple(dil for _, dil in DILATED_GROUPS)

    c8 = jnp.zeros((SUBLANES, d), F32).at[:bsz].set(c)
    mods = _ada(c8, ada_w, ada_b, ADA_COLS)
    kv_mods = _ada(c8, kv_ada_w[None], kv_ada_b[None], d)
    ctab, stab = _rope_tables(positions)

    ffn1_in, ffn1_out = ffn1_w_in.astype(BF16), ffn1_w_out.astype(BF16)
    ffn2_in, ffn2_out = ffn2_w_in.astype(BF16), ffn2_w_out.astype(BF16)
    conv_in, conv_out = conv_w_in.astype(BF16), conv_w_out.astype(BF16)
    k_width = w_kv.shape[1] // 2
    w_kv_p = jnp.concatenate([_permute_pair_columns(w_kv[:, :k_width]), w_kv[:, k_width:]], axis=1)
    w_kv_p = w_kv_p.astype(BF16)[None]
    w_q = _permute_pair_columns(attn_w_q).astype(BF16)
    w_o = attn_w_o.astype(BF16)

    kv_plan = tuple((dil, True, 1.0) for dil in dils) + tuple((dil, False, 1.0) for dil in dils)
    q_plan = tuple((dil, True, LOG2_E * HEAD_DIM ** -0.5) for dil in dils)
    ks = vs = None
    for layer in range(depth):
        if layer == n_conv:
            kv = _proj(x, kv_mods, kv_norm_g[None, None], w_kv_p, ctab, stab, 0, 0, 0, 0, kv_plan)
            ks, vs = kv[:len(dils)], kv[len(dils):]
        x = _ffn(x, mods, norm_g, ffn1_in, ffn1_out, layer, 0, 0)
        mix = None
        if layer < n_conv:
            x = _conv(x, mods, norm_g, conv_in, conv_w, conv_out, layer, 1, 3)
        else:
            j = layer - n_conv
            qs = _proj(x, mods, norm_g, w_q, ctab, stab, layer, 1, j, 3, q_plan)
            mix = (_attn(qs, ks, vs), w_o, j, 5)
        x = _ffn(x, mods, norm_g, ffn2_in, ffn2_out, layer, 2, 6, mix=mix,
                 final_g=final_norm_g[None] if layer == depth - 1 else None)
    return x
```

```python
import functools

import jax
import jax.numpy as jnp
from jax import lax
from jax.experimental import pallas as pl
from jax.experimental.pallas import tpu as pltpu

F32 = jnp.float32
BF16 = jnp.bfloat16

N_MOD = 9
HEAD_DIM = 64
HEADS_PER_GROUP = 8
DILATED_GROUPS = ((128, 1), (512, 4), (2048, 16))
BAND = 128
ROPE_DIM = HEAD_DIM // 4
ROPE_THETA = 500000.0
NORM_EPS = 1e-5
FFN_RES_WEIGHT = 0.5
LOG2_E = 1.4426950408889634
GROUP_WIDTH = HEADS_PER_GROUP * HEAD_DIM

LANES = 128
SUBLANES = 8
VMEM_LIMIT_BYTES = 56 * 1024 * 1024

FFN_ROWS = 1024
FFN_CHUNK = 256
CONV_ROWS = 1024
CONV_CHUNK = 512
PROJ_ROWS = 1024
PROJ_DOT_COLS = 256
ROPE_ROWS = 2048
ATTN_ROWS = 2048
ADA_COLS = 1152


def _params(*sem):
    return pltpu.CompilerParams(dimension_semantics=sem, vmem_limit_bytes=VMEM_LIMIT_BYTES)


def _layer_spec(stacked, layer):
    index = (layer,) + (0,) * (stacked.ndim - 1)
    return pl.BlockSpec((None,) + stacked.shape[1:], lambda *_: index, pipeline_mode=pl.Buffered(1))


def _mod(mods_ref, k, d):
    return mods_ref[pl.ds(pl.program_id(0), 1), pl.ds(k * d, d)]


def _norm_mod(x, g, shift, scale):
    ms = jnp.mean(x * x, axis=-1, keepdims=True)
    y = x * lax.rsqrt(ms + NORM_EPS)
    return (y * g) * (1.0 + scale) + shift


def _ada_kernel(c_ref, w_ref, b_ref, o_ref):
    c = c_ref[...]
    cond = (c * jax.nn.sigmoid(c)).astype(BF16)
    o_ref[0] = jnp.dot(cond, w_ref[0].astype(BF16), preferred_element_type=F32) + b_ref[0]


def _ada(c8, w, b, cols):
    nl, d, n = w.shape
    return pl.pallas_call(
        _ada_kernel,
        grid=(nl, n // cols),
        in_specs=[
            pl.BlockSpec((SUBLANES, d), lambda l, j: (0, 0)),
            pl.BlockSpec((1, d, cols), lambda l, j: (l, 0, j)),
            pl.BlockSpec((1, 1, cols), lambda l, j: (l, 0, j)),
        ],
        out_specs=pl.BlockSpec((1, SUBLANES, cols), lambda l, j: (l, 0, j)),
        out_shape=jax.ShapeDtypeStruct((nl, SUBLANES, n), F32),
        compiler_params=_params("parallel", "parallel"),
        name="ada",
    )(c8, w, b.reshape(nl, 1, n))


def _rope_kernel(pos_ref, inv_ref, sgn_ref, c_ref, s_ref):
    ang = pos_ref[0].astype(F32) * inv_ref[...]
    c_ref[0] = jnp.cos(ang)
    s_ref[0] = jnp.sin(ang) * sgn_ref[...]


_HALF_ROT = ROPE_DIM // 2
PAIR_LANE_PERM = (tuple(range(0, _HALF_ROT)) + tuple(range(HEAD_DIM, HEAD_DIM + _HALF_ROT))
                  + tuple(range(ROPE_DIM, HEAD_DIM)) + tuple(range(_HALF_ROT, ROPE_DIM))
                  + tuple(range(HEAD_DIM + _HALF_ROT, HEAD_DIM + ROPE_DIM)) + tuple(range(HEAD_DIM + ROPE_DIM, LANES)))


def _first_head_lanes(lane):
    return (lane < _HALF_ROT) | ((lane >= ROPE_DIM) & (lane < HEAD_DIM + _HALF_ROT))


def _permute_pair_columns(w):
    pairs = w.reshape(w.shape[:-1] + (w.shape[-1] // LANES, LANES))
    runs, start = [], 0
    for i in range(1, LANES + 1):
        if i == LANES or PAIR_LANE_PERM[i] != PAIR_LANE_PERM[i - 1] + 1:
            runs.append(pairs[..., PAIR_LANE_PERM[start]:PAIR_LANE_PERM[i - 1] + 1])
            start = i
    return jnp.concatenate(runs, axis=-1).reshape(w.shape)


def _rope_tables(positions):
    bsz, seq = positions.shape
    lane = jnp.arange(LANES)
    inv = ROPE_THETA ** (-jnp.arange(0, ROPE_DIM, 2, dtype=F32) / ROPE_DIM)
    first = lane < ROPE_DIM
    second = (lane >= LANES // 2) & (lane < LANES // 2 + ROPE_DIM)
    inv_lane = jnp.where(first | second, inv[lane % _HALF_ROT], 0.0).astype(F32)[None, :]
    sgn_lane = jnp.where(first, -1.0, jnp.where(second, 1.0, 0.0)).astype(F32)[None, :]
    t = min(ROPE_ROWS, seq)
    tab = jax.ShapeDtypeStruct((bsz, seq, LANES), F32)
    return pl.pallas_call(
        _rope_kernel,
        grid=(bsz, seq // t),
        in_specs=[
            pl.BlockSpec((1, t, 1), lambda b, i: (b, i, 0)),
            pl.BlockSpec((1, LANES), lambda b, i: (0, 0)),
            pl.BlockSpec((1, LANES), lambda b, i: (0, 0)),
        ],
        out_specs=[pl.BlockSpec((1, t, LANES), lambda b, i: (b, i, 0))] * 2,
        out_shape=[tab, tab],
        compiler_params=_params("parallel", "parallel"),
        name="rope",
    )(positions.reshape(bsz, seq, 1), inv_lane, sgn_lane)


def _apply_rope(t, c, s):
    return t * c + pltpu.roll(t, LANES // 2, 1) * s


def _ffn_kernel(*refs, g_row, mod_off, chunks, mix_gate_off, final_norm, n_casts):
    x_ref, mods_ref, g_ref, win_ref, wout_ref = refs[:5]
    rest = list(refs[5:])
    a_ref, wo_ref = (rest.pop(0), rest.pop(0)) if mix_gate_off is not None else (None, None)
    fg_ref = rest.pop(0) if final_norm else None
    cast_src = [rest.pop(0) for _ in range(n_casts)]
    o_ref = rest.pop(0)
    cast_dst = [rest.pop(0) for _ in range(n_casts)]
    (h_scr,) = rest
    for src, dst in zip(cast_src, cast_dst):
        dst[...] = src[...].astype(BF16)
    d = x_ref.shape[-1]
    f = wout_ref.shape[0]
    x = x_ref[0]
    if mix_gate_off is not None:
        mix = jnp.dot(a_ref[0], wo_ref[...], preferred_element_type=F32)
        x = x + (1.0 + _mod(mods_ref, mix_gate_off, d)) * mix
    o_ref[0] = x
    h = _norm_mod(x, g_ref[g_row:g_row + 1, :], _mod(mods_ref, mod_off, d), _mod(mods_ref, mod_off + 1, d))
    h_scr[...] = h.astype(BF16)
    gate = FFN_RES_WEIGHT * (1.0 + _mod(mods_ref, mod_off + 2, d))
    for lo, hi in chunks:
        a = jnp.dot(h_scr[...], win_ref[:, lo:hi], preferred_element_type=F32)
        b = jnp.dot(h_scr[...], win_ref[:, f + lo:f + hi], preferred_element_type=F32)
        act = ((a * jax.nn.sigmoid(a)) * b).astype(BF16)
        o_ref[0] += gate * jnp.dot(act, wout_ref[lo:hi, :], preferred_element_type=F32)
    if final_norm:
        y = o_ref[0]
        ms = jnp.mean(y * y, axis=-1, keepdims=True)
        o_ref[0] = (y * lax.rsqrt(ms + NORM_EPS)) * fg_ref[...]


def _cast_job(stacked, layer, n_steps, tiles):
    rows, cols = stacked.shape[1:]
    n = n_steps
    while rows % (n * 2 * SUBLANES):
        n //= 2
    blk = rows // n
    src = pl.BlockSpec((None, blk, cols), lambda b, i: (layer, jnp.minimum(b * tiles + i, n - 1), 0))
    dst = pl.BlockSpec((blk, cols), lambda b, i: (jnp.minimum(b * tiles + i, n - 1), 0))
    return src, dst, jax.ShapeDtypeStruct((rows, cols), BF16)


def _ffn(x, mods, norm_g, w_in, w_out, layer, w_layer, g_row, mod_off, mix=None, final_g=None, casts=()):
    bsz, seq, d = x.shape
    f = w_out.shape[1]
    chunks = tuple((lo, min(lo + FFN_CHUNK, f)) for lo in range(0, f, FFN_CHUNK))
    tm = min(FFN_ROWS, seq)
    tiles = seq // tm
    in_specs = [
        pl.BlockSpec((1, tm, d), lambda b, i: (b, i, 0)),
        _layer_spec(mods, layer),
        _layer_spec(norm_g, layer),
        _layer_spec(w_in, w_layer),
        _layer_spec(w_out, w_layer),
    ]
    args = [x, mods, norm_g, w_in, w_out]
    mix_gate_off = None
    if mix is not None:
        a, w_o, j, mix_gate_off = mix
        in_specs += [pl.BlockSpec((1, tm, a.shape[-1]), lambda b, i: (b, i, 0)), _layer_spec(w_o, j)]
        args += [a, w_o]
    if final_g is not None:
        in_specs.append(pl.BlockSpec((1, d), lambda b, i: (0, 0)))
        args.append(final_g)
    out_specs = [pl.BlockSpec((1, tm, d), lambda b, i: (b, i, 0))]
    out_shape = [jax.ShapeDtypeStruct(x.shape, F32)]
    for stacked, index in casts:
        src, dst, shape = _cast_job(stacked, index, bsz * tiles, tiles)
        in_specs.append(src)
        args.append(stacked)
        out_specs.append(dst)
        out_shape.append(shape)
    out = pl.pallas_call(
        functools.partial(_ffn_kernel, g_row=g_row, mod_off=mod_off, chunks=chunks, mix_gate_off=mix_gate_off,
                          final_norm=final_g is not None, n_casts=len(casts)),
        grid=(bsz, tiles),
        in_specs=in_specs,
        out_specs=out_specs,
        out_shape=out_shape,
        scratch_shapes=[pltpu.VMEM((tm, d), BF16)],
        compiler_params=_params("arbitrary", "arbitrary"),
        name="ffn",
    )(*args)
    return out[0], out[1:]


def _conv_kernel(x_ref, mods_ref, g_ref, win_ref, cw_ref, wout_ref, o_ref, h_scr, vbuf, carry,
                 *, g_row, mod_off, n_chunks):
    tm, d = h_scr.shape
    ck = d // n_chunks

    @pl.when(pl.program_id(1) == 0)
    def _():
        carry[...] = jnp.zeros_like(carry)

    h = _norm_mod(x_ref[0], g_ref[g_row:g_row + 1, :], _mod(mods_ref, mod_off, d), _mod(mods_ref, mod_off + 1, d))
    h_scr[...] = h.astype(BF16)
    gate = 1.0 + _mod(mods_ref, mod_off + 2, d)
    for j in range(n_chunks):
        cols = [slice(k * d + j * ck, k * d + (j + 1) * ck) for k in range(3)]
        v = (jnp.dot(h_scr[...], win_ref[:, cols[1]], preferred_element_type=F32)
             * jnp.dot(h_scr[...], win_ref[:, cols[2]], preferred_element_type=F32))
        vbuf[0:SUBLANES, :] = carry[j]
        vbuf[SUBLANES:, :] = v
        carry[j] = vbuf[tm:, :]
        cw = cw_ref[:, j * ck:(j + 1) * ck]
        conv = (cw[0:1] * vbuf[SUBLANES - 2:tm + SUBLANES - 2, :]
                + cw[1:2] * vbuf[SUBLANES - 1:tm + SUBLANES - 1, :]
                + cw[2:3] * v)
        bg = jnp.dot(h_scr[...], win_ref[:, cols[0]], preferred_element_type=F32)
        y = jnp.dot((bg * conv).astype(BF16), wout_ref[j * ck:(j + 1) * ck, :], preferred_element_type=F32)
        if j == 0:
            o_ref[0] = x_ref[0] + gate * y
        else:
            o_ref[0] += gate * y


def _conv(x, mods, norm_g, w_in, conv_w, w_out, layer, w_layer, g_row, mod_off):
    bsz, seq, d = x.shape
    ck = CONV_CHUNK
    nc = d // ck
    tm = min(CONV_ROWS, seq)
    return pl.pallas_call(
        functools.partial(_conv_kernel, g_row=g_row, mod_off=mod_off, n_chunks=nc),
        grid=(bsz, seq // tm),
        in_specs=[
            pl.BlockSpec((1, tm, d), lambda b, i: (b, i, 0)),
            _layer_spec(mods, layer),
            _layer_spec(norm_g, layer),
            _layer_spec(w_in, w_layer),
            _layer_spec(conv_w, layer),
            _layer_spec(w_out, w_layer),
        ],
        out_specs=pl.BlockSpec((1, tm, d), lambda b, i: (b, i, 0)),
        out_shape=jax.ShapeDtypeStruct(x.shape, F32),
        scratch_shapes=[
            pltpu.VMEM((tm, d), BF16),
            pltpu.VMEM((tm + SUBLANES, ck), F32),
            pltpu.VMEM((nc, SUBLANES, ck), F32),
        ],
        compiler_params=_params("arbitrary", "arbitrary"),
        name="conv",
    )(x, mods, norm_g, w_in, conv_w, w_out)


def _proj_kernel(*refs, g_row, shift_off, plan):
    x_ref, mods_ref, g_ref, w_ref, c_ref, s_ref = refs[:6]
    outs = refs[6:6 + len(plan)]
    zs, h_scr = refs[6 + len(plan):]
    tm, d = x_ref.shape[1:]
    h = _norm_mod(x_ref[0], g_ref[g_row:g_row + 1, :], _mod(mods_ref, shift_off, d),
                  _mod(mods_ref, shift_off + 1, d))
    h_scr[...] = h.astype(BF16)
    slices_per_dot = PROJ_DOT_COLS // LANES
    for j in sorted(range(len(plan)), key=lambda n: plan[n][:2], reverse=True):
        dil, rope, mult = plan[j]
        for half in range(GROUP_WIDTH // PROJ_DOT_COLS):
            col0 = j * GROUP_WIDTH + half * PROJ_DOT_COLS
            z = jnp.dot(h_scr[...], w_ref[:, col0:col0 + PROJ_DOT_COLS], preferred_element_type=F32)
            for k in range(slices_per_dot):
                plane = half * slices_per_dot + k
                lanes = slice(plane * LANES, (plane + 1) * LANES)
                zk = z[:, k * LANES:(k + 1) * LANES]
                if rope:
                    zk = _apply_rope(zk, c_ref[0], s_ref[0])
                if mult != 1.0:
                    zk = zk * mult
                if dil == 1:
                    outs[j][0, 0, :, lanes] = zk.astype(BF16)
                else:
                    zs[plane] = zk
                    for r in range(dil):
                        outs[j][0, r, :, lanes] = zs[plane, pl.ds(r, tm // dil, stride=dil), :].astype(BF16)


def _proj(x, mods, norm_g, w, ctab, stab, layer, g_row, w_index, shift_off, plan):
    bsz, seq, d = x.shape
    tm = min(PROJ_ROWS, seq)
    out_shape = [jax.ShapeDtypeStruct((bsz, dil, seq // dil, GROUP_WIDTH), BF16) for dil, _, _ in plan]
    out_specs = [pl.BlockSpec((1, dil, tm // dil, GROUP_WIDTH), lambda b, i: (b, 0, i, 0)) for dil, _, _ in plan]
    return pl.pallas_call(
        functools.partial(_proj_kernel, g_row=g_row, shift_off=shift_off, plan=plan),
        grid=(bsz, seq // tm),
        in_specs=[
            pl.BlockSpec((1, tm, d), lambda b, i: (b, i, 0)),
            _layer_spec(mods, layer),
            _layer_spec(norm_g, layer),
            _layer_spec(w, w_index),
            pl.BlockSpec((1, tm, LANES), lambda b, i: (b, i, 0)),
            pl.BlockSpec((1, tm, LANES), lambda b, i: (b, i, 0)),
        ],
        out_specs=out_specs,
        out_shape=out_shape,
        scratch_shapes=[pltpu.VMEM((GROUP_WIDTH // LANES, tm, LANES), F32), pltpu.VMEM((tm, d), BF16)],
        compiler_params=_params("parallel", "parallel"),
        name="proj",
    )(x, mods, norm_g, w, ctab, stab)


def _attn_kernel(*refs, dils, rows):
    ng = len(dils)
    ins = refs[:5 * ng]
    o_ref = refs[5 * ng]
    scr = refs[5 * ng + 1:]
    *strided, dense = sorted(range(ng), key=lambda g: dils[g], reverse=True)
    assert dils[dense] == 1
    accs = dict(zip(strided, scr[0:ng - 1]))
    mxs = dict(zip(strided, scr[ng - 1:2 * ng - 2]))
    dens = dict(zip(strided, scr[2 * ng - 2:3 * ng - 3]))
    bias_ref = scr[3 * ng - 3]
    at_start = (pl.program_id(1) == 0).astype(jnp.int32)

    row = lax.broadcasted_iota(jnp.int32, (2 * BAND, 2 * BAND), 0) % BAND
    col = lax.broadcasted_iota(jnp.int32, (2 * BAND, 2 * BAND), 1)
    valid = (col >= row) & (col <= row + BAND)
    bias_ref[0] = jnp.where(valid, 0.0, -jnp.inf)
    bias_ref[1] = jnp.where(valid & (col >= BAND), 0.0, -jnp.inf)
    lane = lax.broadcasted_iota(jnp.int32, (BAND, LANES), 1)
    q_first = _first_head_lanes(lane)
    lo = lane < HEAD_DIM
    ones = jnp.ones((2 * BAND, LANES), BF16)

    def block_stats(g, r, j):
        q_ref, k_ref, kh_ref, v_ref, vh_ref = ins[5 * g:5 * g + 5]
        cur = slice(j * BAND, (j + 1) * BAND)
        q = q_ref[0, r, cur, :]
        zero = jnp.zeros_like(q)
        q2 = jnp.concatenate([jnp.where(q_first, q, zero), jnp.where(q_first, zero, q)], axis=0)
        if j == 0:
            k = jnp.concatenate([kh_ref[0, r], k_ref[0, r, cur, :]], axis=0)
            v = jnp.concatenate([vh_ref[0, r], v_ref[0, r, cur, :]], axis=0)
            bias = bias_ref[at_start]
        else:
            both = slice((j - 1) * BAND, (j + 1) * BAND)
            k = k_ref[0, r, both, :]
            v = v_ref[0, r, both, :]
            bias = bias_ref[0]
        s = lax.dot_general(q2, k, (((1,), (1,)), ((), ())), preferred_element_type=F32) + bias
        m = jnp.max(s, axis=-1, keepdims=True)
        p = jnp.exp2(s - m).astype(BF16)
        pv = jnp.dot(p, jnp.concatenate([v, ones], axis=1), preferred_element_type=F32)
        return (jnp.where(lo, pv[:BAND, :LANES], pv[BAND:, :LANES]), jnp.where(lo, m[:BAND], m[BAND:]),
                jnp.where(lo, pv[:BAND, LANES:], pv[BAND:, LANES:]))

    for g in strided:
        dil = dils[g]
        for r in range(dil):
            for j in range(rows // dil // BAND):
                idx = pl.ds(j * BAND * dil + r, BAND, stride=dil)
                accs[g][idx, :], mxs[g][idx, :], dens[g][idx, :] = block_stats(g, r, j)
    for j in range(rows // BAND):
        sl = slice(j * BAND, (j + 1) * BAND)
        parts = [block_stats(dense, 0, j)] + [(accs[g][sl, :], mxs[g][sl, :], dens[g][sl, :]) for g in strided]
        m_all = functools.reduce(jnp.maximum, [m for _, m, _ in parts])
        num = den = None
        for acc, m, l in parts:
            a = jnp.exp2(m - m_all)
            num = a * acc if num is None else num + a * acc
            den = a * l if den is None else den + a * l
        o_ref[0, sl, :] = (num / den).astype(BF16)


def _attn(qs, ks, vs):
    bsz = qs[0].shape[0]
    dils = tuple(q.shape[1] for q in qs)
    seq = qs[0].shape[1] * qs[0].shape[2]
    rows = min(ATTN_ROWS, seq)
    n_pairs = GROUP_WIDTH // LANES
    in_specs, args = [], []
    for q, k, v in zip(qs, ks, vs):
        dil = q.shape[1]
        blk = rows // dil
        per = blk // BAND
        main = pl.BlockSpec((1, dil, blk, LANES), lambda b, i, hp: (b, 0, i, hp))
        halo = pl.BlockSpec((1, dil, BAND, LANES),
                            lambda b, i, hp, per=per: (b, 0, jnp.maximum(i * per - 1, 0), hp))
        in_specs += [main, main, halo, main, halo]
        args += [q, k, k, v, v]
    stat = [pltpu.VMEM((rows, LANES), F32)] * (3 * (len(dils) - 1))
    return pl.pallas_call(
        functools.partial(_attn_kernel, dils=dils, rows=rows),
        grid=(bsz, seq // rows, n_pairs),
        in_specs=in_specs,
        out_specs=pl.BlockSpec((1, rows, LANES), lambda b, i, hp: (b, i, hp)),
        out_shape=jax.ShapeDtypeStruct((bsz, seq, GROUP_WIDTH), BF16),
        scratch_shapes=stat + [pltpu.VMEM((2, 2 * BAND, 2 * BAND), F32)],
        compiler_params=_params("parallel", "parallel", "parallel"),
        name="attn",
    )(*args)


def kernel(x, c, positions, norm_g, ada_w, ada_b, ffn1_w_in, ffn1_w_out, ffn2_w_in, ffn2_w_out, conv_w_in, conv_w,
           conv_w_out, kv_norm_g, kv_ada_w, kv_ada_b, w_kv, attn_w_q, attn_w_o, final_norm_g):
    bsz, seq, d = x.shape
    depth = ada_w.shape[0]
    n_conv = conv_w_in.shape[0]
    assert all(win // dil == BAND for win, dil in DILATED_GROUPS)
    assert bsz <= SUBLANES and seq % ATTN_ROWS == 0
    dils = tuple(dil for _, dil in DILATED_GROUPS)

    c8 = jnp.zeros((SUBLANES, d), F32).at[:bsz].set(c)
    mods = _ada(c8, ada_w, ada_b, ADA_COLS)
    kv_mods = _ada(c8, kv_ada_w[None], kv_ada_b[None], d)
    ctab, stab = _rope_tables(positions)

    w_in, w_out = ffn1_w_in[:1].astype(BF16), ffn1_w_out[:1].astype(BF16)
    k_width = w_kv.shape[1] // 2
    w_kv_p = jnp.concatenate([_permute_pair_columns(w_kv[:, :k_width]), w_kv[:, k_width:]], axis=1)
    w_kv_p = w_kv_p.astype(BF16)[None]
    w_q = _permute_pair_columns(attn_w_q).astype(BF16)
    w_o = attn_w_o.astype(BF16)

    kv_plan = tuple((dil, True, 1.0) for dil in dils) + tuple((dil, False, 1.0) for dil in dils)
    q_plan = tuple((dil, True, LOG2_E * HEAD_DIM ** -0.5) for dil in dils)
    ks = vs = None
    for layer in range(depth):
        if layer == n_conv:
            kv = _proj(x, kv_mods, kv_norm_g[None, None], w_kv_p, ctab, stab, 0, 0, 0, 0, kv_plan)
            ks, vs = kv[:len(dils)], kv[len(dils):]
        casts = [(ffn2_w_in, layer), (ffn2_w_out, layer)]
        if layer < n_conv:
            casts += [(conv_w_in, layer), (conv_w_out, layer)]
        x, cast = _ffn(x, mods, norm_g, w_in, w_out, layer, 0, 0, 0, casts=casts)
        w_in, w_out = cast[0][None], cast[1][None]
        mix = None
        if layer < n_conv:
            x = _conv(x, mods, norm_g, cast[2][None], conv_w, cast[3][None], layer, 0, 1, 3)
        else:
            j = layer - n_conv
            qs = _proj(x, mods, norm_g, w_q, ctab, stab, layer, 1, j, 3, q_plan)
            mix = (_attn(qs, ks, vs), w_o, j, 5)
        last = layer == depth - 1
        x, cast = _ffn(x, mods, norm_g, w_in, w_out, layer, 0, 2, 6, mix=mix,
                       final_g=final_norm_g[None] if last else None,
                       casts=() if last else [(ffn1_w_in, layer + 1), (ffn1_w_out, layer + 1)])
        if not last:
            w_in, w_out = cast[0][None], cast[1][None]
    return x
```

```python
import functools

import jax
import jax.numpy as jnp
from jax import lax
from jax.experimental import pallas as pl
from jax.experimental.pallas import tpu as pltpu

F32 = jnp.float32
BF16 = jnp.bfloat16

N_MOD = 9
HEAD_DIM = 64
HEADS_PER_GROUP = 8
DILATED_GROUPS = ((128, 1), (512, 4), (2048, 16))
BAND = 128
ROPE_DIM = HEAD_DIM // 4
ROPE_THETA = 500000.0
NORM_EPS = 1e-5
FFN_RES_WEIGHT = 0.5
LOG2_E = 1.4426950408889634
GROUP_WIDTH = HEADS_PER_GROUP * HEAD_DIM

LANES = 128
SUBLANES = 8
VMEM_LIMIT_BYTES = 56 * 1024 * 1024

FFN_ROWS = 1024
FFN_CHUNK = 256
CONV_ROWS = 1024
CONV_CHUNK = 512
PROJ_ROWS = 1024
PROJ_DOT_COLS = 256
RELAYOUT_STRIDE = 4
ROPE_ROWS = 2048
ATTN_ROWS = 2048
ADA_COLS = 1152


def _params(*sem):
    return pltpu.CompilerParams(dimension_semantics=sem, vmem_limit_bytes=VMEM_LIMIT_BYTES)


def _layer_spec(stacked, layer):
    index = (layer,) + (0,) * (stacked.ndim - 1)
    return pl.BlockSpec((None,) + stacked.shape[1:], lambda *_: index, pipeline_mode=pl.Buffered(1))


def _mod(mods_ref, k, d):
    return mods_ref[pl.ds(pl.program_id(0), 1), pl.ds(k * d, d)]


def _norm_mod(x, g, shift, scale):
    ms = jnp.mean(x * x, axis=-1, keepdims=True)
    y = x * lax.rsqrt(ms + NORM_EPS)
    return (y * g) * (1.0 + scale) + shift


def _ada_kernel(c_ref, w_ref, b_ref, o_ref):
    c = c_ref[...]
    cond = (c * jax.nn.sigmoid(c)).astype(BF16)
    o_ref[0] = jnp.dot(cond, w_ref[0].astype(BF16), preferred_element_type=F32) + b_ref[0]


def _ada(c8, w, b, cols):
    nl, d, n = w.shape
    return pl.pallas_call(
        _ada_kernel,
        grid=(nl, n // cols),
        in_specs=[
            pl.BlockSpec((SUBLANES, d), lambda l, j: (0, 0)),
            pl.BlockSpec((1, d, cols), lambda l, j: (l, 0, j)),
            pl.BlockSpec((1, 1, cols), lambda l, j: (l, 0, j)),
        ],
        out_specs=pl.BlockSpec((1, SUBLANES, cols), lambda l, j: (l, 0, j)),
        out_shape=jax.ShapeDtypeStruct((nl, SUBLANES, n), F32),
        compiler_params=_params("parallel", "parallel"),
        name="ada",
    )(c8, w, b.reshape(nl, 1, n))


def _rope_kernel(*refs, n_casts):
    pos_ref, inv_ref, sgn_ref = refs[:3]
    cast_src = refs[3:3 + n_casts]
    c_ref, s_ref = refs[3 + n_casts:5 + n_casts]
    cast_dst = refs[5 + n_casts:]
    ang = pos_ref[0].astype(F32) * inv_ref[...]
    c_ref[0] = jnp.cos(ang)
    s_ref[0] = jnp.sin(ang) * sgn_ref[...]
    for src, dst in zip(cast_src, cast_dst):
        dst[...] = src[...].astype(BF16)


_HALF_ROT = ROPE_DIM // 2
PAIR_LANE_PERM = (tuple(range(0, _HALF_ROT)) + tuple(range(HEAD_DIM, HEAD_DIM + _HALF_ROT))
                  + tuple(range(ROPE_DIM, HEAD_DIM)) + tuple(range(_HALF_ROT, ROPE_DIM))
                  + tuple(range(HEAD_DIM + _HALF_ROT, HEAD_DIM + ROPE_DIM)) + tuple(range(HEAD_DIM + ROPE_DIM, LANES)))


def _first_head_lanes(lane):
    return (lane < _HALF_ROT) | ((lane >= ROPE_DIM) & (lane < HEAD_DIM + _HALF_ROT))


def _permute_pair_columns(w):
    pairs = w.reshape(w.shape[:-1] + (w.shape[-1] // LANES, LANES))
    runs, start = [], 0
    for i in range(1, LANES + 1):
        if i == LANES or PAIR_LANE_PERM[i] != PAIR_LANE_PERM[i - 1] + 1:
            runs.append(pairs[..., PAIR_LANE_PERM[start]:PAIR_LANE_PERM[i - 1] + 1])
            start = i
    return jnp.concatenate(runs, axis=-1).reshape(w.shape)


def _rope_tables(positions, casts=()):
    bsz, seq = positions.shape
    lane = jnp.arange(LANES)
    inv = ROPE_THETA ** (-jnp.arange(0, ROPE_DIM, 2, dtype=F32) / ROPE_DIM)
    first = lane < ROPE_DIM
    second = (lane >= LANES // 2) & (lane < LANES // 2 + ROPE_DIM)
    inv_lane = jnp.where(first | second, inv[lane % _HALF_ROT], 0.0).astype(F32)[None, :]
    sgn_lane = jnp.where(first, -1.0, jnp.where(second, 1.0, 0.0)).astype(F32)[None, :]
    t = min(ROPE_ROWS, seq)
    tab = jax.ShapeDtypeStruct((bsz, seq, LANES), F32)
    jobs = [_cast_job(stacked, index, bsz * (seq // t), seq // t) for stacked, index in casts]
    out = pl.pallas_call(
        functools.partial(_rope_kernel, n_casts=len(jobs)),
        grid=(bsz, seq // t),
        in_specs=[
            pl.BlockSpec((1, t, 1), lambda b, i: (b, i, 0)),
            pl.BlockSpec((1, LANES), lambda b, i: (0, 0)),
            pl.BlockSpec((1, LANES), lambda b, i: (0, 0)),
        ] + [src for src, _, _ in jobs],
        out_specs=[pl.BlockSpec((1, t, LANES), lambda b, i: (b, i, 0))] * 2 + [dst for _, dst, _ in jobs],
        out_shape=[tab, tab] + [shape for _, _, shape in jobs],
        compiler_params=_params("arbitrary", "arbitrary"),
        name="rope",
    )(positions.reshape(bsz, seq, 1), inv_lane, sgn_lane, *[stacked for stacked, _ in casts])
    return out[0], out[1], out[2:]


def _apply_rope(t, c, s):
    return t * c + pltpu.roll(t, LANES // 2, 1) * s


def _ffn_kernel(*refs, g_row, mod_off, chunks, mix_gate_off, final_norm, n_casts):
    x_ref, mods_ref, g_ref, win_ref, wout_ref = refs[:5]
    rest = list(refs[5:])
    a_ref, wo_ref = (rest.pop(0), rest.pop(0)) if mix_gate_off is not None else (None, None)
    fg_ref = rest.pop(0) if final_norm else None
    cast_src = [rest.pop(0) for _ in range(n_casts)]
    o_ref = rest.pop(0)
    cast_dst = [rest.pop(0) for _ in range(n_casts)]
    (h_scr,) = rest
    for src, dst in zip(cast_src, cast_dst):
        dst[...] = src[...].astype(BF16)
    d = x_ref.shape[-1]
    f = wout_ref.shape[0]
    x = x_ref[0]
    if mix_gate_off is not None:
        mix = jnp.dot(a_ref[0], wo_ref[...], preferred_element_type=F32)
        x = x + (1.0 + _mod(mods_ref, mix_gate_off, d)) * mix
    o_ref[0] = x
    h = _norm_mod(x, g_ref[g_row:g_row + 1, :], _mod(mods_ref, mod_off, d), _mod(mods_ref, mod_off + 1, d))
    h_scr[...] = h.astype(BF16)
    gate = FFN_RES_WEIGHT * (1.0 + _mod(mods_ref, mod_off + 2, d))
    for lo, hi in chunks:
        a = jnp.dot(h_scr[...], win_ref[:, lo:hi], preferred_element_type=F32)
        b = jnp.dot(h_scr[...], win_ref[:, f + lo:f + hi], preferred_element_type=F32)
        act = ((a * jax.nn.sigmoid(a)) * b).astype(BF16)
        o_ref[0] += gate * jnp.dot(act, wout_ref[lo:hi, :], preferred_element_type=F32)
    if final_norm:
        y = o_ref[0]
        ms = jnp.mean(y * y, axis=-1, keepdims=True)
        o_ref[0] = (y * lax.rsqrt(ms + NORM_EPS)) * fg_ref[...]


def _cast_job(stacked, layer, n_steps, tiles):
    rows, cols = stacked.shape[1:]
    n = n_steps
    while rows % (n * 2 * SUBLANES):
        n //= 2
    blk = rows // n
    src = pl.BlockSpec((None, blk, cols), lambda b, i: (layer, jnp.minimum(b * tiles + i, n - 1), 0))
    dst = pl.BlockSpec((blk, cols), lambda b, i: (jnp.minimum(b * tiles + i, n - 1), 0))
    return src, dst, jax.ShapeDtypeStruct((rows, cols), BF16)


def _ffn(x, mods, norm_g, w_in, w_out, layer, w_layer, g_row, mod_off, mix=None, final_g=None, casts=()):
    bsz, seq, d = x.shape
    f = w_out.shape[1]
    chunks = tuple((lo, min(lo + FFN_CHUNK, f)) for lo in range(0, f, FFN_CHUNK))
    tm = min(FFN_ROWS, seq)
    tiles = seq // tm
    in_specs = [
        pl.BlockSpec((1, tm, d), lambda b, i: (b, i, 0)),
        _layer_spec(mods, layer),
        _layer_spec(norm_g, layer),
        _layer_spec(w_in, w_layer),
        _layer_spec(w_out, w_layer),
    ]
    args = [x, mods, norm_g, w_in, w_out]
    mix_gate_off = None
    if mix is not None:
        a, w_o, j, mix_gate_off = mix
        in_specs += [pl.BlockSpec((1, tm, a.shape[-1]), lambda b, i: (b, i, 0)), _layer_spec(w_o, j)]
        args += [a, w_o]
    if final_g is not None:
        in_specs.append(pl.BlockSpec((1, d), lambda b, i: (0, 0)))
        args.append(final_g)
    out_specs = [pl.BlockSpec((1, tm, d), lambda b, i: (b, i, 0))]
    out_shape = [jax.ShapeDtypeStruct(x.shape, F32)]
    for stacked, index in casts:
        src, dst, shape = _cast_job(stacked, index, bsz * tiles, tiles)
        in_specs.append(src)
        args.append(stacked)
        out_specs.append(dst)
        out_shape.append(shape)
    out = pl.pallas_call(
        functools.partial(_ffn_kernel, g_row=g_row, mod_off=mod_off, chunks=chunks, mix_gate_off=mix_gate_off,
                          final_norm=final_g is not None, n_casts=len(casts)),
        grid=(bsz, tiles),
        in_specs=in_specs,
        out_specs=out_specs,
        out_shape=out_shape,
        scratch_shapes=[pltpu.VMEM((tm, d), BF16)],
        compiler_params=_params("arbitrary", "arbitrary"),
        name="ffn",
    )(*args)
    return out[0], out[1:]


def _conv_kernel(x_ref, mods_ref, g_ref, win_ref, cw_ref, wout_ref, o_ref, h_scr, vbuf, carry,
                 *, g_row, mod_off, n_chunks):
    tm, d = h_scr.shape
    ck = d // n_chunks

    @pl.when(pl.program_id(1) == 0)
    def _():
        carry[...] = jnp.zeros_like(carry)

    h = _norm_mod(x_ref[0], g_ref[g_row:g_row + 1, :], _mod(mods_ref, mod_off, d), _mod(mods_ref, mod_off + 1, d))
    h_scr[...] = h.astype(BF16)
    gate = 1.0 + _mod(mods_ref, mod_off + 2, d)
    for j in range(n_chunks):
        cols = [slice(k * d + j * ck, k * d + (j + 1) * ck) for k in range(3)]
        v = (jnp.dot(h_scr[...], win_ref[:, cols[1]], preferred_element_type=F32)
             * jnp.dot(h_scr[...], win_ref[:, cols[2]], preferred_element_type=F32))
        vbuf[0:SUBLANES, :] = carry[j]
        vbuf[SUBLANES:, :] = v
        carry[j] = vbuf[tm:, :]
        cw = cw_ref[:, j * ck:(j + 1) * ck]
        conv = (cw[0:1] * vbuf[SUBLANES - 2:tm + SUBLANES - 2, :]
                + cw[1:2] * vbuf[SUBLANES - 1:tm + SUBLANES - 1, :]
                + cw[2:3] * v)
        bg = jnp.dot(h_scr[...], win_ref[:, cols[0]], preferred_element_type=F32)
        y = jnp.dot((bg * conv).astype(BF16), wout_ref[j * ck:(j + 1) * ck, :], preferred_element_type=F32)
        if j == 0:
            o_ref[0] = x_ref[0] + gate * y
        else:
            o_ref[0] += gate * y


def _conv(x, mods, norm_g, w_in, conv_w, w_out, layer, w_layer, g_row, mod_off):
    bsz, seq, d = x.shape
    ck = CONV_CHUNK
    nc = d // ck
    tm = min(CONV_ROWS, seq)
    return pl.pallas_call(
        functools.partial(_conv_kernel, g_row=g_row, mod_off=mod_off, n_chunks=nc),
        grid=(bsz, seq // tm),
        in_specs=[
            pl.BlockSpec((1, tm, d), lambda b, i: (b, i, 0)),
            _layer_spec(mods, layer),
            _layer_spec(norm_g, layer),
            _layer_spec(w_in, w_layer),
            _layer_spec(conv_w, layer),
            _layer_spec(w_out, w_layer),
        ],
        out_specs=pl.BlockSpec((1, tm, d), lambda b, i: (b, i, 0)),
        out_shape=jax.ShapeDtypeStruct(x.shape, F32),
        scratch_shapes=[
            pltpu.VMEM((tm, d), BF16),
            pltpu.VMEM((tm + SUBLANES, ck), F32),
            pltpu.VMEM((nc, SUBLANES, ck), F32),
        ],
        compiler_params=_params("arbitrary", "arbitrary"),
        name="conv",
    )(x, mods, norm_g, w_in, conv_w, w_out)


def _proj_kernel(*refs, g_row, shift_off, plan):
    x_ref, mods_ref, g_ref, w_ref, c_ref, s_ref = refs[:6]
    outs = refs[6:6 + len(plan)]
    zs, zs2, h_scr = refs[6 + len(plan):]
    tm, d = x_ref.shape[1:]
    h = _norm_mod(x_ref[0], g_ref[g_row:g_row + 1, :], _mod(mods_ref, shift_off, d),
                  _mod(mods_ref, shift_off + 1, d))
    h_scr[...] = h.astype(BF16)
    slices_per_dot = PROJ_DOT_COLS // LANES
    for j in sorted(range(len(plan)), key=lambda n: plan[n][:2], reverse=True):
        dil, rope, mult = plan[j]
        for half in range(GROUP_WIDTH // PROJ_DOT_COLS):
            col0 = j * GROUP_WIDTH + half * PROJ_DOT_COLS
            z = jnp.dot(h_scr[...], w_ref[:, col0:col0 + PROJ_DOT_COLS], preferred_element_type=F32)
            for k in range(slices_per_dot):
                plane = half * slices_per_dot + k
                lanes = slice(plane * LANES, (plane + 1) * LANES)
                zk = z[:, k * LANES:(k + 1) * LANES]
                if rope:
                    zk = _apply_rope(zk, c_ref[0], s_ref[0])
                if mult != 1.0:
                    zk = zk * mult
                if dil == 1:
                    outs[j][0, 0, :, lanes] = zk.astype(BF16)
                else:
                    zs[plane] = zk
                    if dil <= RELAYOUT_STRIDE:
                        for r in range(dil):
                            outs[j][0, r, :, lanes] = zs[plane, pl.ds(r, tm // dil, stride=dil), :].astype(BF16)
                    else:
                        s1, s2 = RELAYOUT_STRIDE, dil // RELAYOUT_STRIDE
                        run = tm // s1
                        for r1 in range(s1):
                            zs2[plane, r1 * run:(r1 + 1) * run, :] = zs[plane, pl.ds(r1, run, stride=s1), :]
                        for r in range(dil):
                            r1, r2 = r % s1, r // s1
                            rows = zs2[plane, pl.ds(r1 * run + r2, tm // dil, stride=s2), :]
                            outs[j][0, r, :, lanes] = rows.astype(BF16)


def _proj(x, mods, norm_g, w, ctab, stab, layer, g_row, w_index, shift_off, plan):
    bsz, seq, d = x.shape
    tm = min(PROJ_ROWS, seq)
    out_shape = [jax.ShapeDtypeStruct((bsz, dil, seq // dil, GROUP_WIDTH), BF16) for dil, _, _ in plan]
    out_specs = [pl.BlockSpec((1, dil, tm // dil, GROUP_WIDTH), lambda b, i: (b, 0, i, 0)) for dil, _, _ in plan]
    return pl.pallas_call(
        functools.partial(_proj_kernel, g_row=g_row, shift_off=shift_off, plan=plan),
        grid=(bsz, seq // tm),
        in_specs=[
            pl.BlockSpec((1, tm, d), lambda b, i: (b, i, 0)),
            _layer_spec(mods, layer),
            _layer_spec(norm_g, layer),
            _layer_spec(w, w_index),
            pl.BlockSpec((1, tm, LANES), lambda b, i: (b, i, 0)),
            pl.BlockSpec((1, tm, LANES), lambda b, i: (b, i, 0)),
        ],
        out_specs=out_specs,
        out_shape=out_shape,
        scratch_shapes=[pltpu.VMEM((GROUP_WIDTH // LANES, tm, LANES), F32)] * 2 + [pltpu.VMEM((tm, d), BF16)],
        compiler_params=_params("parallel", "parallel"),
        name="proj",
    )(x, mods, norm_g, w, ctab, stab)


def _attn_kernel(*refs, dils, rows):
    ng = len(dils)
    ins = refs[:5 * ng]
    o_ref = refs[5 * ng]
    scr = refs[5 * ng + 1:]
    *strided, dense = sorted(range(ng), key=lambda g: dils[g], reverse=True)
    assert dils[dense] == 1
    accs = dict(zip(strided, scr[0:ng - 1]))
    mxs = dict(zip(strided, scr[ng - 1:2 * ng - 2]))
    dens = dict(zip(strided, scr[2 * ng - 2:3 * ng - 3]))
    bias_ref = scr[3 * ng - 3]
    at_start = (pl.program_id(1) == 0).astype(jnp.int32)

    row = lax.broadcasted_iota(jnp.int32, (2 * BAND, 2 * BAND), 0) % BAND
    col = lax.broadcasted_iota(jnp.int32, (2 * BAND, 2 * BAND), 1)
    valid = (col >= row) & (col <= row + BAND)
    bias_ref[0] = jnp.where(valid, 0.0, -jnp.inf)
    bias_ref[1] = jnp.where(valid & (col >= BAND), 0.0, -jnp.inf)
    lane = lax.broadcasted_iota(jnp.int32, (BAND, LANES), 1)
    q_first = _first_head_lanes(lane)
    lo = lane < HEAD_DIM
    ones = jnp.ones((2 * BAND, LANES), BF16)

    def block_stats(g, r, j):
        q_ref, k_ref, kh_ref, v_ref, vh_ref = ins[5 * g:5 * g + 5]
        cur = slice(j * BAND, (j + 1) * BAND)
        q = q_ref[0, r, cur, :]
        zero = jnp.zeros_like(q)
        q2 = jnp.concatenate([jnp.where(q_first, q, zero), jnp.where(q_first, zero, q)], axis=0)
        if j == 0:
            k = jnp.concatenate([kh_ref[0, r], k_ref[0, r, cur, :]], axis=0)
            v = jnp.concatenate([vh_ref[0, r], v_ref[0, r, cur, :]], axis=0)
            bias = bias_ref[at_start]
        else:
            both = slice((j - 1) * BAND, (j + 1) * BAND)
            k = k_ref[0, r, both, :]
            v = v_ref[0, r, both, :]
            bias = bias_ref[0]
        s = lax.dot_general(q2, k, (((1,), (1,)), ((), ())), preferred_element_type=F32) + bias
        m = jnp.max(s, axis=-1, keepdims=True)
        p = jnp.exp2(s - m).astype(BF16)
        pv = jnp.dot(p, jnp.concatenate([v, ones], axis=1), preferred_element_type=F32)
        return (jnp.where(lo, pv[:BAND, :LANES], pv[BAND:, :LANES]), jnp.where(lo, m[:BAND], m[BAND:]),
                jnp.where(lo, pv[:BAND, LANES:], pv[BAND:, LANES:]))

    for g in strided:
        dil = dils[g]
        for r in range(dil):
            for j in range(rows // dil // BAND):
                idx = pl.ds(j * BAND * dil + r, BAND, stride=dil)
                accs[g][idx, :], mxs[g][idx, :], dens[g][idx, :] = block_stats(g, r, j)
    for j in range(rows // BAND):
        sl = slice(j * BAND, (j + 1) * BAND)
        parts = [block_stats(dense, 0, j)] + [(accs[g][sl, :], mxs[g][sl, :], dens[g][sl, :]) for g in strided]
        m_all = functools.reduce(jnp.maximum, [m for _, m, _ in parts])
        num = den = None
        for acc, m, l in parts:
            a = jnp.exp2(m - m_all)
            num = a * acc if num is None else num + a * acc
            den = a * l if den is None else den + a * l
        o_ref[0, sl, :] = (num / den).astype(BF16)


def _attn(qs, ks, vs):
    bsz = qs[0].shape[0]
    dils = tuple(q.shape[1] for q in qs)
    seq = qs[0].shape[1] * qs[0].shape[2]
    rows = min(ATTN_ROWS, seq)
    n_pairs = GROUP_WIDTH // LANES
    in_specs, args = [], []
    for q, k, v in zip(qs, ks, vs):
        dil = q.shape[1]
        blk = rows // dil
        per = blk // BAND
        main = pl.BlockSpec((1, dil, blk, LANES), lambda b, i, hp: (b, 0, i, hp))
        halo = pl.BlockSpec((1, dil, BAND, LANES),
                            lambda b, i, hp, per=per: (b, 0, jnp.maximum(i * per - 1, 0), hp))
        in_specs += [main, main, halo, main, halo]
        args += [q, k, k, v, v]
    stat = [pltpu.VMEM((rows, LANES), F32)] * (3 * (len(dils) - 1))
    return pl.pallas_call(
        functools.partial(_attn_kernel, dils=dils, rows=rows),
        grid=(bsz, seq // rows, n_pairs),
        in_specs=in_specs,
        out_specs=pl.BlockSpec((1, rows, LANES), lambda b, i, hp: (b, i, hp)),
        out_shape=jax.ShapeDtypeStruct((bsz, seq, GROUP_WIDTH), BF16),
        scratch_shapes=stat + [pltpu.VMEM((2, 2 * BAND, 2 * BAND), F32)],
        compiler_params=_params("parallel", "parallel", "parallel"),
        name="attn",
    )(*args)


def kernel(x, c, positions, norm_g, ada_w, ada_b, ffn1_w_in, ffn1_w_out, ffn2_w_in, ffn2_w_out, conv_w_in, conv_w,
           conv_w_out, kv_norm_g, kv_ada_w, kv_ada_b, w_kv, attn_w_q, attn_w_o, final_norm_g):
    bsz, seq, d = x.shape
    depth = ada_w.shape[0]
    n_conv = conv_w_in.shape[0]
    assert all(win // dil == BAND for win, dil in DILATED_GROUPS)
    assert bsz <= SUBLANES and seq % ATTN_ROWS == 0
    dils = tuple(dil for _, dil in DILATED_GROUPS)

    c8 = jnp.zeros((SUBLANES, d), F32).at[:bsz].set(c)
    mods = _ada(c8, ada_w, ada_b, ADA_COLS)
    kv_mods = _ada(c8, kv_ada_w[None], kv_ada_b[None], d)
    ctab, stab, (w_in, w_out) = _rope_tables(positions, casts=[(ffn1_w_in, 0), (ffn1_w_out, 0)])
    w_in, w_out = w_in[None], w_out[None]

    k_width = w_kv.shape[1] // 2
    w_kv_p = jnp.concatenate([_permute_pair_columns(w_kv[:, :k_width]), w_kv[:, k_width:]], axis=1)
    w_kv_p = w_kv_p.astype(BF16)[None]
    w_q = _permute_pair_columns(attn_w_q).astype(BF16)
    w_o = attn_w_o.astype(BF16)

    kv_plan = tuple((dil, True, 1.0) for dil in dils) + tuple((dil, False, 1.0) for dil in dils)
    q_plan = tuple((dil, True, LOG2_E * HEAD_DIM ** -0.5) for dil in dils)
    ks = vs = None
    for layer in range(depth):
        if layer == n_conv:
            kv = _proj(x, kv_mods, kv_norm_g[None, None], w_kv_p, ctab, stab, 0, 0, 0, 0, kv_plan)
            ks, vs = kv[:len(dils)], kv[len(dils):]
        casts = [(ffn2_w_in, layer), (ffn2_w_out, layer)]
        if layer < n_conv:
            casts += [(conv_w_in, layer), (conv_w_out, layer)]
        x, cast = _ffn(x, mods, norm_g, w_in, w_out, layer, 0, 0, 0, casts=casts)
        w_in, w_out = cast[0][None], cast[1][None]
        mix = None
        if layer < n_conv:
            x = _conv(x, mods, norm_g, cast[2][None], conv_w, cast[3][None], layer, 0, 1, 3)
        else:
            j = layer - n_conv
            qs = _proj(x, mods, norm_g, w_q, ctab, stab, layer, 1, j, 3, q_plan)
            mix = (_attn(qs, ks, vs), w_o, j, 5)
        last = layer == depth - 1
        x, cast = _ffn(x, mods, norm_g, w_in, w_out, layer, 0, 2, 6, mix=mix,
                       final_g=final_norm_g[None] if last else None,
                       casts=() if last else [(ffn1_w_in, layer + 1), (ffn1_w_out, layer + 1)])
        if not last:
            w_in, w_out = cast[0][None], cast[1][None]
    return x
```

```python
import functools

import jax
import jax.numpy as jnp
from jax import lax
from jax.experimental import pallas as pl
from jax.experimental.pallas import tpu as pltpu

F32 = jnp.float32
BF16 = jnp.bfloat16

N_MOD = 9
HEAD_DIM = 64
HEADS_PER_GROUP = 8
DILATED_GROUPS = ((128, 1), (512, 4), (2048, 16))
BAND = 128
ROPE_DIM = HEAD_DIM // 4
ROPE_THETA = 500000.0
NORM_EPS = 1e-5
FFN_RES_WEIGHT = 0.5
LOG2_E = 1.4426950408889634
GROUP_WIDTH = HEADS_PER_GROUP * HEAD_DIM

LANES = 128
SUBLANES = 8
VMEM_LIMIT_BYTES = 56 * 1024 * 1024

FFN_ROWS = 1024
FFN_CHUNK = 256
CONV_ROWS = 1024
CONV_CHUNK = 512
PROJ_ROWS = 1024
PROJ_DOT_COLS = 256
RELAYOUT_STRIDE = 4
ROPE_ROWS = 2048
ATTN_ROWS = 2048
ADA_COLS = 1152


def _params(*sem):
    return pltpu.CompilerParams(dimension_semantics=sem, vmem_limit_bytes=VMEM_LIMIT_BYTES)


def _layer_spec(stacked, layer):
    index = (layer,) + (0,) * (stacked.ndim - 1)
    return pl.BlockSpec((None,) + stacked.shape[1:], lambda *_: index, pipeline_mode=pl.Buffered(1))


def _mod(mods_ref, k, d):
    return mods_ref[pl.ds(pl.program_id(0), 1), pl.ds(k * d, d)]


def _norm_mod(x, g, shift, scale):
    ms = jnp.mean(x * x, axis=-1, keepdims=True)
    y = x * lax.rsqrt(ms + NORM_EPS)
    return (y * g) * (1.0 + scale) + shift


def _ada_kernel(c_ref, w_ref, b_ref, o_ref):
    c = c_ref[...]
    cond = (c * jax.nn.sigmoid(c)).astype(BF16)
    o_ref[0] = jnp.dot(cond, w_ref[0].astype(BF16), preferred_element_type=F32) + b_ref[0]


def _ada(c8, w, b, cols):
    nl, d, n = w.shape
    return pl.pallas_call(
        _ada_kernel,
        grid=(nl, n // cols),
        in_specs=[
            pl.BlockSpec((SUBLANES, d), lambda l, j: (0, 0)),
            pl.BlockSpec((1, d, cols), lambda l, j: (l, 0, j)),
            pl.BlockSpec((1, 1, cols), lambda l, j: (l, 0, j)),
        ],
        out_specs=pl.BlockSpec((1, SUBLANES, cols), lambda l, j: (l, 0, j)),
        out_shape=jax.ShapeDtypeStruct((nl, SUBLANES, n), F32),
        compiler_params=_params("parallel", "parallel"),
        name="ada",
    )(c8, w, b.reshape(nl, 1, n))


def _rope_kernel(*refs, n_casts):
    pos_ref, inv_ref, sgn_ref = refs[:3]
    cast_src = refs[3:3 + n_casts]
    c_ref, s_ref = refs[3 + n_casts:5 + n_casts]
    cast_dst = refs[5 + n_casts:]
    ang = pos_ref[0].astype(F32) * inv_ref[...]
    c_ref[0] = jnp.cos(ang)
    s_ref[0] = jnp.sin(ang) * sgn_ref[...]
    for src, dst in zip(cast_src, cast_dst):
        _cast_block(src, dst, 0)


_HALF_ROT = ROPE_DIM // 2


def _first_head_lanes(lane):
    return (lane < _HALF_ROT) | ((lane >= ROPE_DIM) & (lane < HEAD_DIM + _HALF_ROT))


def _cast_block(src, dst, permuted_cols):
    rows, cols = src.shape
    if permuted_cols:
        lane = lax.broadcasted_iota(jnp.int32, (rows, LANES), 1)
        shift = HEAD_DIM - _HALF_ROT
        to_b_r1 = (lane >= _HALF_ROT) & (lane < ROPE_DIM)
        to_a_r2 = (lane >= HEAD_DIM) & (lane < HEAD_DIM + _HALF_ROT)
        for g in range(permuted_cols // LANES):
            lanes = slice(g * LANES, (g + 1) * LANES)
            v = src[:, lanes]
            v = jnp.where(to_b_r1, pltpu.roll(v, LANES - shift, 1), jnp.where(to_a_r2, pltpu.roll(v, shift, 1), v))
            dst[:, lanes] = v.astype(BF16)
    if permuted_cols < cols:
        dst[:, permuted_cols:] = src[:, permuted_cols:].astype(BF16)


def _rope_tables(positions, casts=()):
    bsz, seq = positions.shape
    lane = jnp.arange(LANES)
    inv = ROPE_THETA ** (-jnp.arange(0, ROPE_DIM, 2, dtype=F32) / ROPE_DIM)
    first = lane < ROPE_DIM
    second = (lane >= LANES // 2) & (lane < LANES // 2 + ROPE_DIM)
    inv_lane = jnp.where(first | second, inv[lane % _HALF_ROT], 0.0).astype(F32)[None, :]
    sgn_lane = jnp.where(first, -1.0, jnp.where(second, 1.0, 0.0)).astype(F32)[None, :]
    t = min(ROPE_ROWS, seq)
    tab = jax.ShapeDtypeStruct((bsz, seq, LANES), F32)
    jobs = [_cast_job(stacked, index, bsz * (seq // t), seq // t) for stacked, index in casts]
    out = pl.pallas_call(
        functools.partial(_rope_kernel, n_casts=len(jobs)),
        grid=(bsz, seq // t),
        in_specs=[
            pl.BlockSpec((1, t, 1), lambda b, i: (b, i, 0)),
            pl.BlockSpec((1, LANES), lambda b, i: (0, 0)),
            pl.BlockSpec((1, LANES), lambda b, i: (0, 0)),
        ] + [src for src, _, _ in jobs],
        out_specs=[pl.BlockSpec((1, t, LANES), lambda b, i: (b, i, 0))] * 2 + [dst for _, dst, _ in jobs],
        out_shape=[tab, tab] + [shape for _, _, shape in jobs],
        compiler_params=_params("arbitrary", "arbitrary"),
        name="rope",
    )(positions.reshape(bsz, seq, 1), inv_lane, sgn_lane, *[stacked for stacked, _ in casts])
    return out[0], out[1], out[2:]


def _apply_rope(t, c, s):
    return t * c + pltpu.roll(t, LANES // 2, 1) * s


def _ffn_kernel(*refs, g_row, mod_off, chunks, mix_gate_off, final_norm, cast_perm):
    x_ref, mods_ref, g_ref, win_ref, wout_ref = refs[:5]
    rest = list(refs[5:])
    a_ref, wo_ref = (rest.pop(0), rest.pop(0)) if mix_gate_off is not None else (None, None)
    fg_ref = rest.pop(0) if final_norm else None
    cast_src = [rest.pop(0) for _ in cast_perm]
    o_ref = rest.pop(0)
    cast_dst = [rest.pop(0) for _ in cast_perm]
    (h_scr,) = rest
    for src, dst, permuted_cols in zip(cast_src, cast_dst, cast_perm):
        _cast_block(src, dst, permuted_cols)
    d = x_ref.shape[-1]
    f = wout_ref.shape[0]
    x = x_ref[0]
    if mix_gate_off is not None:
        mix = jnp.dot(a_ref[0], wo_ref[...], preferred_element_type=F32)
        x = x + (1.0 + _mod(mods_ref, mix_gate_off, d)) * mix
    o_ref[0] = x
    h = _norm_mod(x, g_ref[g_row:g_row + 1, :], _mod(mods_ref, mod_off, d), _mod(mods_ref, mod_off + 1, d))
    h_scr[...] = h.astype(BF16)
    gate = FFN_RES_WEIGHT * (1.0 + _mod(mods_ref, mod_off + 2, d))
    for lo, hi in chunks:
        a = jnp.dot(h_scr[...], win_ref[:, lo:hi], preferred_element_type=F32)
        b = jnp.dot(h_scr[...], win_ref[:, f + lo:f + hi], preferred_element_type=F32)
        act = ((a * jax.nn.sigmoid(a)) * b).astype(BF16)
        o_ref[0] += gate * jnp.dot(act, wout_ref[lo:hi, :], preferred_element_type=F32)
    if final_norm:
        y = o_ref[0]
        ms = jnp.mean(y * y, axis=-1, keepdims=True)
        o_ref[0] = (y * lax.rsqrt(ms + NORM_EPS)) * fg_ref[...]


def _cast_job(stacked, layer, n_steps, tiles):
    rows, cols = stacked.shape[1:]
    n = n_steps
    while rows % (n * 2 * SUBLANES):
        n //= 2
    blk = rows // n
    src = pl.BlockSpec((None, blk, cols), lambda b, i: (layer, jnp.minimum(b * tiles + i, n - 1), 0))
    dst = pl.BlockSpec((blk, cols), lambda b, i: (jnp.minimum(b * tiles + i, n - 1), 0))
    return src, dst, jax.ShapeDtypeStruct((rows, cols), BF16)


def _ffn(x, mods, norm_g, w_in, w_out, layer, w_layer, g_row, mod_off, mix=None, final_g=None, casts=()):
    bsz, seq, d = x.shape
    f = w_out.shape[1]
    chunks = tuple((lo, min(lo + FFN_CHUNK, f)) for lo in range(0, f, FFN_CHUNK))
    tm = min(FFN_ROWS, seq)
    tiles = seq // tm
    in_specs = [
        pl.BlockSpec((1, tm, d), lambda b, i: (b, i, 0)),
        _layer_spec(mods, layer),
        _layer_spec(norm_g, layer),
        _layer_spec(w_in, w_layer),
        _layer_spec(w_out, w_layer),
    ]
    args = [x, mods, norm_g, w_in, w_out]
    mix_gate_off = None
    if mix is not None:
        a, w_o, j, mix_gate_off = mix
        in_specs += [pl.BlockSpec((1, tm, a.shape[-1]), lambda b, i: (b, i, 0)), _layer_spec(w_o, j)]
        args += [a, w_o]
    if final_g is not None:
        in_specs.append(pl.BlockSpec((1, d), lambda b, i: (0, 0)))
        args.append(final_g)
    out_specs = [pl.BlockSpec((1, tm, d), lambda b, i: (b, i, 0))]
    out_shape = [jax.ShapeDtypeStruct(x.shape, F32)]
    for stacked, index, *_ in casts:
        src, dst, shape = _cast_job(stacked, index, bsz * tiles, tiles)
        in_specs.append(src)
        args.append(stacked)
        out_specs.append(dst)
        out_shape.append(shape)
    cast_perm = tuple(job[2] if len(job) > 2 else 0 for job in casts)
    out = pl.pallas_call(
        functools.partial(_ffn_kernel, g_row=g_row, mod_off=mod_off, chunks=chunks, mix_gate_off=mix_gate_off,
                          final_norm=final_g is not None, cast_perm=cast_perm),
        grid=(bsz, tiles),
        in_specs=in_specs,
        out_specs=out_specs,
        out_shape=out_shape,
        scratch_shapes=[pltpu.VMEM((tm, d), BF16)],
        compiler_params=_params("arbitrary", "arbitrary"),
        name="ffn",
    )(*args)
    return out[0], out[1:]


def _conv_kernel(x_ref, mods_ref, g_ref, win_ref, cw_ref, wout_ref, o_ref, h_scr, vbuf, carry,
                 *, g_row, mod_off, n_chunks):
    tm, d = h_scr.shape
    ck = d // n_chunks

    @pl.when(pl.program_id(1) == 0)
    def _():
        carry[...] = jnp.zeros_like(carry)

    h = _norm_mod(x_ref[0], g_ref[g_row:g_row + 1, :], _mod(mods_ref, mod_off, d), _mod(mods_ref, mod_off + 1, d))
    h_scr[...] = h.astype(BF16)
    gate = 1.0 + _mod(mods_ref, mod_off + 2, d)
    for j in range(n_chunks):
        cols = [slice(k * d + j * ck, k * d + (j + 1) * ck) for k in range(3)]
        v = (jnp.dot(h_scr[...], win_ref[:, cols[1]], preferred_element_type=F32)
             * jnp.dot(h_scr[...], win_ref[:, cols[2]], preferred_element_type=F32))
        vbuf[0:SUBLANES, :] = carry[j]
        vbuf[SUBLANES:, :] = v
        carry[j] = vbuf[tm:, :]
        cw = cw_ref[:, j * ck:(j + 1) * ck]
        conv = (cw[0:1] * vbuf[SUBLANES - 2:tm + SUBLANES - 2, :]
                + cw[1:2] * vbuf[SUBLANES - 1:tm + SUBLANES - 1, :]
                + cw[2:3] * v)
        bg = jnp.dot(h_scr[...], win_ref[:, cols[0]], preferred_element_type=F32)
        y = jnp.dot((bg * conv).astype(BF16), wout_ref[j * ck:(j + 1) * ck, :], preferred_element_type=F32)
        if j == 0:
            o_ref[0] = x_ref[0] + gate * y
        else:
            o_ref[0] += gate * y


def _conv(x, mods, norm_g, w_in, conv_w, w_out, layer, w_layer, g_row, mod_off):
    bsz, seq, d = x.shape
    ck = CONV_CHUNK
    nc = d // ck
    tm = min(CONV_ROWS, seq)
    return pl.pallas_call(
        functools.partial(_conv_kernel, g_row=g_row, mod_off=mod_off, n_chunks=nc),
        grid=(bsz, seq // tm),
        in_specs=[
            pl.BlockSpec((1, tm, d), lambda b, i: (b, i, 0)),
            _layer_spec(mods, layer),
            _layer_spec(norm_g, layer),
            _layer_spec(w_in, w_layer),
            _layer_spec(conv_w, layer),
            _layer_spec(w_out, w_layer),
        ],
        out_specs=pl.BlockSpec((1, tm, d), lambda b, i: (b, i, 0)),
        out_shape=jax.ShapeDtypeStruct(x.shape, F32),
        scratch_shapes=[
            pltpu.VMEM((tm, d), BF16),
            pltpu.VMEM((tm + SUBLANES, ck), F32),
            pltpu.VMEM((nc, SUBLANES, ck), F32),
        ],
        compiler_params=_params("arbitrary", "arbitrary"),
        name="conv",
    )(x, mods, norm_g, w_in, conv_w, w_out)


def _proj_kernel(*refs, g_row, shift_off, plan):
    x_ref, mods_ref, g_ref, w_ref, c_ref, s_ref = refs[:6]
    outs = refs[6:6 + len(plan)]
    zs, zs2, h_scr = refs[6 + len(plan):]
    tm, d = x_ref.shape[1:]
    h = _norm_mod(x_ref[0], g_ref[g_row:g_row + 1, :], _mod(mods_ref, shift_off, d),
                  _mod(mods_ref, shift_off + 1, d))
    h_scr[...] = h.astype(BF16)
    slices_per_dot = PROJ_DOT_COLS // LANES
    for j in sorted(range(len(plan)), key=lambda n: plan[n][:2], reverse=True):
        dil, rope, mult = plan[j]
        for half in range(GROUP_WIDTH // PROJ_DOT_COLS):
            col0 = j * GROUP_WIDTH + half * PROJ_DOT_COLS
            z = jnp.dot(h_scr[...], w_ref[:, col0:col0 + PROJ_DOT_COLS], preferred_element_type=F32)
            for k in range(slices_per_dot):
                plane = half * slices_per_dot + k
                lanes = slice(plane * LANES, (plane + 1) * LANES)
                zk = z[:, k * LANES:(k + 1) * LANES]
                if rope:
                    zk = _apply_rope(zk, c_ref[0], s_ref[0])
                if mult != 1.0:
                    zk = zk * mult
                if dil == 1:
                    outs[j][0, 0, :, lanes] = zk.astype(BF16)
                else:
                    zs[plane] = zk
                    if dil <= RELAYOUT_STRIDE:
                        for r in range(dil):
                            outs[j][0, r, :, lanes] = zs[plane, pl.ds(r, tm // dil, stride=dil), :].astype(BF16)
                    else:
                        s1, s2 = RELAYOUT_STRIDE, dil // RELAYOUT_STRIDE
                        run = tm // s1
                        for r1 in range(s1):
                            zs2[plane, r1 * run:(r1 + 1) * run, :] = zs[plane, pl.ds(r1, run, stride=s1), :]
                        for r in range(dil):
                            r1, r2 = r % s1, r // s1
                            rows = zs2[plane, pl.ds(r1 * run + r2, tm // dil, stride=s2), :]
                            outs[j][0, r, :, lanes] = rows.astype(BF16)


def _proj(x, mods, norm_g, w, ctab, stab, layer, g_row, w_index, shift_off, plan):
    bsz, seq, d = x.shape
    tm = min(PROJ_ROWS, seq)
    out_shape = [jax.ShapeDtypeStruct((bsz, dil, seq // dil, GROUP_WIDTH), BF16) for dil, _, _ in plan]
    out_specs = [pl.BlockSpec((1, dil, tm // dil, GROUP_WIDTH), lambda b, i: (b, 0, i, 0)) for dil, _, _ in plan]
    return pl.pallas_call(
        functools.partial(_proj_kernel, g_row=g_row, shift_off=shift_off, plan=plan),
        grid=(bsz, seq // tm),
        in_specs=[
            pl.BlockSpec((1, tm, d), lambda b, i: (b, i, 0)),
            _layer_spec(mods, layer),
            _layer_spec(norm_g, layer),
            _layer_spec(w, w_index),
            pl.BlockSpec((1, tm, LANES), lambda b, i: (b, i, 0)),
            pl.BlockSpec((1, tm, LANES), lambda b, i: (b, i, 0)),
        ],
        out_specs=out_specs,
        out_shape=out_shape,
        scratch_shapes=[pltpu.VMEM((GROUP_WIDTH // LANES, tm, LANES), F32)] * 2 + [pltpu.VMEM((tm, d), BF16)],
        compiler_params=_params("parallel", "parallel"),
        name="proj",
    )(x, mods, norm_g, w, ctab, stab)


def _attn_kernel(*refs, dils, rows):
    ng = len(dils)
    ins = refs[:5 * ng]
    o_ref = refs[5 * ng]
    scr = refs[5 * ng + 1:]
    *strided, dense = sorted(range(ng), key=lambda g: dils[g], reverse=True)
    assert dils[dense] == 1
    accs = dict(zip(strided, scr[0:ng - 1]))
    mxs = dict(zip(strided, scr[ng - 1:2 * ng - 2]))
    dens = dict(zip(strided, scr[2 * ng - 2:3 * ng - 3]))
    bias_ref = scr[3 * ng - 3]
    at_start = (pl.program_id(1) == 0).astype(jnp.int32)

    row = lax.broadcasted_iota(jnp.int32, (2 * BAND, 2 * BAND), 0) % BAND
    col = lax.broadcasted_iota(jnp.int32, (2 * BAND, 2 * BAND), 1)
    valid = (col >= row) & (col <= row + BAND)
    bias_ref[0] = jnp.where(valid, 0.0, -jnp.inf)
    bias_ref[1] = jnp.where(valid & (col >= BAND), 0.0, -jnp.inf)
    lane = lax.broadcasted_iota(jnp.int32, (BAND, LANES), 1)
    q_first = _first_head_lanes(lane)
    lo = lane < HEAD_DIM
    ones = jnp.ones((2 * BAND, LANES), BF16)

    def block_stats(g, r, j):
        q_ref, k_ref, kh_ref, v_ref, vh_ref = ins[5 * g:5 * g + 5]
        cur = slice(j * BAND, (j + 1) * BAND)
        q = q_ref[0, r, cur, :]
        zero = jnp.zeros_like(q)
        q2 = jnp.concatenate([jnp.where(q_first, q, zero), jnp.where(q_first, zero, q)], axis=0)
        if j == 0:
            k = jnp.concatenate([kh_ref[0, r], k_ref[0, r, cur, :]], axis=0)
            v = jnp.concatenate([vh_ref[0, r], v_ref[0, r, cur, :]], axis=0)
            bias = bias_ref[at_start]
        else:
            both = slice((j - 1) * BAND, (j + 1) * BAND)
            k = k_ref[0, r, both, :]
            v = v_ref[0, r, both, :]
            bias = bias_ref[0]
        s = lax.dot_general(q2, k, (((1,), (1,)), ((), ())), preferred_element_type=F32) + bias
        m = jnp.max(s, axis=-1, keepdims=True)
        p = jnp.exp2(s - m).astype(BF16)
        pv = jnp.dot(p, jnp.concatenate([v, ones], axis=1), preferred_element_type=F32)
        return (jnp.where(lo, pv[:BAND, :LANES], pv[BAND:, :LANES]), jnp.where(lo, m[:BAND], m[BAND:]),
                jnp.where(lo, pv[:BAND, LANES:], pv[BAND:, LANES:]))

    for g in strided:
        dil = dils[g]
        for r in range(dil):
            for j in range(rows // dil // BAND):
                idx = pl.ds(j * BAND * dil + r, BAND, stride=dil)
                accs[g][idx, :], mxs[g][idx, :], dens[g][idx, :] = block_stats(g, r, j)
    for j in range(rows // BAND):
        sl = slice(j * BAND, (j + 1) * BAND)
        parts = [block_stats(dense, 0, j)] + [(accs[g][sl, :], mxs[g][sl, :], dens[g][sl, :]) for g in strided]
        m_all = functools.reduce(jnp.maximum, [m for _, m, _ in parts])
        num = den = None
        for acc, m, l in parts:
            a = jnp.exp2(m - m_all)
            num = a * acc if num is None else num + a * acc
            den = a * l if den is None else den + a * l
        o_ref[0, sl, :] = (num / den).astype(BF16)


def _attn(qs, ks, vs):
    bsz = qs[0].shape[0]
    dils = tuple(q.shape[1] for q in qs)
    seq = qs[0].shape[1] * qs[0].shape[2]
    rows = min(ATTN_ROWS, seq)
    n_pairs = GROUP_WIDTH // LANES
    in_specs, args = [], []
    for q, k, v in zip(qs, ks, vs):
        dil = q.shape[1]
        blk = rows // dil
        per = blk // BAND
        main = pl.BlockSpec((1, dil, blk, LANES), lambda b, i, hp: (b, 0, i, hp))
        halo = pl.BlockSpec((1, dil, BAND, LANES),
                            lambda b, i, hp, per=per: (b, 0, jnp.maximum(i * per - 1, 0), hp))
        in_specs += [main, main, halo, main, halo]
        args += [q, k, k, v, v]
    stat = [pltpu.VMEM((rows, LANES), F32)] * (3 * (len(dils) - 1))
    return pl.pallas_call(
        functools.partial(_attn_kernel, dils=dils, rows=rows),
        grid=(bsz, seq // rows, n_pairs),
        in_specs=in_specs,
        out_specs=pl.BlockSpec((1, rows, LANES), lambda b, i, hp: (b, i, hp)),
        out_shape=jax.ShapeDtypeStruct((bsz, seq, GROUP_WIDTH), BF16),
        scratch_shapes=stat + [pltpu.VMEM((2, 2 * BAND, 2 * BAND), F32)],
        compiler_params=_params("parallel", "parallel", "parallel"),
        name="attn",
    )(*args)


def kernel(x, c, positions, norm_g, ada_w, ada_b, ffn1_w_in, ffn1_w_out, ffn2_w_in, ffn2_w_out, conv_w_in, conv_w,
           conv_w_out, kv_norm_g, kv_ada_w, kv_ada_b, w_kv, attn_w_q, attn_w_o, final_norm_g):
    bsz, seq, d = x.shape
    depth = ada_w.shape[0]
    n_conv = conv_w_in.shape[0]
    assert all(win // dil == BAND for win, dil in DILATED_GROUPS)
    assert bsz <= SUBLANES and seq % ATTN_ROWS == 0
    dils = tuple(dil for _, dil in DILATED_GROUPS)

    c8 = jnp.zeros((SUBLANES, d), F32).at[:bsz].set(c)
    mods = _ada(c8, ada_w, ada_b, ADA_COLS)
    kv_mods = _ada(c8, kv_ada_w[None], kv_ada_b[None], d)
    ctab, stab, (w_in, w_out) = _rope_tables(positions, casts=[(ffn1_w_in, 0), (ffn1_w_out, 0)])
    w_in, w_out = w_in[None], w_out[None]

    attn_casts = [(w_kv[None], 0, w_kv.shape[1] // 2)]
    attn_casts += [(attn_w_q, j, attn_w_q.shape[2]) for j in range(depth - n_conv)]
    attn_casts += [(attn_w_o, j) for j in range(depth - n_conv)]
    assert n_conv >= 1
    w_kv_p = w_q = w_o = None

    kv_plan = tuple((dil, True, 1.0) for dil in dils) + tuple((dil, False, 1.0) for dil in dils)
    q_plan = tuple((dil, True, LOG2_E * HEAD_DIM ** -0.5) for dil in dils)
    ks = vs = None
    for layer in range(depth):
        if layer == n_conv:
            kv = _proj(x, kv_mods, kv_norm_g[None, None], w_kv_p[None], ctab, stab, 0, 0, 0, 0, kv_plan)
            ks, vs = kv[:len(dils)], kv[len(dils):]
        casts = [(ffn2_w_in, layer), (ffn2_w_out, layer)]
        if layer < n_conv:
            casts += [(conv_w_in, layer), (conv_w_out, layer)]
        x, cast = _ffn(x, mods, norm_g, w_in, w_out, layer, 0, 0, 0, casts=casts)
        w_in, w_out = cast[0][None], cast[1][None]
        mix = None
        if layer < n_conv:
            x = _conv(x, mods, norm_g, cast[2][None], conv_w, cast[3][None], layer, 0, 1, 3)
        else:
            j = layer - n_conv
            qs = _proj(x, mods, norm_g, w_q[j][None], ctab, stab, layer, 1, 0, 3, q_plan)
            mix = (_attn(qs, ks, vs), w_o[j][None], 0, 5)
        last = layer == depth - 1
        casts = [] if last else [(ffn1_w_in, layer + 1), (ffn1_w_out, layer + 1)]
        if layer == n_conv - 1:
            casts += attn_casts
        x, cast = _ffn(x, mods, norm_g, w_in, w_out, layer, 0, 2, 6, mix=mix,
                       final_g=final_norm_g[None] if last else None, casts=casts)
        if not last:
            w_in, w_out = cast[0][None], cast[1][None]
        if layer == n_conv - 1:
            n_attn = depth - n_conv
            w_kv_p, w_q, w_o = cast[2], cast[3:3 + n_attn], cast[3 + n_attn:3 + 2 * n_attn]
    return x
```

```python
import functools

import jax
import jax.numpy as jnp
from jax import lax
from jax.experimental import pallas as pl
from jax.experimental.pallas import tpu as pltpu

F32 = jnp.float32
BF16 = jnp.bfloat16

N_MOD = 9
HEAD_DIM = 64
HEADS_PER_GROUP = 8
DILATED_GROUPS = ((128, 1), (512, 4), (2048, 16))
BAND = 128
ROPE_DIM = HEAD_DIM // 4
ROPE_THETA = 500000.0
NORM_EPS = 1e-5
FFN_RES_WEIGHT = 0.5
LOG2_E = 1.4426950408889634
GROUP_WIDTH = HEADS_PER_GROUP * HEAD_DIM

LANES = 128
SUBLANES = 8
VMEM_LIMIT_BYTES = 56 * 1024 * 1024

FFN_ROWS = 1024
FFN_CHUNK = 256
CONV_ROWS = 1024
CONV_CHUNK = 512
PROJ_ROWS = 1024
PROJ_DOT_COLS = 256
RELAYOUT_STRIDE = 4
ROPE_ROWS = 2048
ATTN_ROWS = 2048
ADA_COLS = 1152


def _params(*sem):
    return pltpu.CompilerParams(dimension_semantics=sem, vmem_limit_bytes=VMEM_LIMIT_BYTES)


def _layer_spec(stacked, layer):
    index = (layer,) + (0,) * (stacked.ndim - 1)
    return pl.BlockSpec((None,) + stacked.shape[1:], lambda *_: index, pipeline_mode=pl.Buffered(1))


def _mod(mods_ref, k, d):
    return mods_ref[pl.ds(pl.program_id(0), 1), pl.ds(k * d, d)]


def _norm_mod(x, g, shift, scale):
    ms = jnp.mean(x * x, axis=-1, keepdims=True)
    y = x * lax.rsqrt(ms + NORM_EPS)
    return (y * g) * (1.0 + scale) + shift


def _ada_kernel(c_ref, w_ref, b_ref, o_ref):
    c = c_ref[...]
    cond = (c * jax.nn.sigmoid(c)).astype(BF16)
    o_ref[0] = jnp.dot(cond, w_ref[0].astype(BF16), preferred_element_type=F32) + b_ref[0]


def _ada(c8, w, b, cols):
    nl, d, n = w.shape
    return pl.pallas_call(
        _ada_kernel,
        grid=(nl, n // cols),
        in_specs=[
            pl.BlockSpec((SUBLANES, d), lambda l, j: (0, 0)),
            pl.BlockSpec((1, d, cols), lambda l, j: (l, 0, j)),
            pl.BlockSpec((1, 1, cols), lambda l, j: (l, 0, j)),
        ],
        out_specs=pl.BlockSpec((1, SUBLANES, cols), lambda l, j: (l, 0, j)),
        out_shape=jax.ShapeDtypeStruct((nl, SUBLANES, n), F32),
        compiler_params=_params("parallel", "parallel"),
        name="ada",
    )(c8, w, b.reshape(nl, 1, n))


def _rope_kernel(*refs, n_casts):
    pos_ref, inv_ref, expand_ref, base_ref, sgn_ref = refs[:5]
    cast_src = refs[5:5 + n_casts]
    c_ref, s_ref = refs[5 + n_casts:7 + n_casts]
    cast_dst = refs[7 + n_casts:]
    ang = inv_ref[...] * pos_ref[0].astype(F32)
    expand = expand_ref[...]
    hi = lax.Precision.HIGHEST
    cos = jnp.dot(jnp.cos(ang).T, expand, precision=hi, preferred_element_type=F32)
    sin = jnp.dot(jnp.sin(ang).T, expand, precision=hi, preferred_element_type=F32)
    c_ref[0] = cos + base_ref[...]
    s_ref[0] = sin * sgn_ref[...]
    for src, dst in zip(cast_src, cast_dst):
        _cast_block(src, dst, 0)


_HALF_ROT = ROPE_DIM // 2


def _first_head_lanes(lane):
    return (lane < _HALF_ROT) | ((lane >= ROPE_DIM) & (lane < HEAD_DIM + _HALF_ROT))


def _cast_block(src, dst, permuted_cols):
    rows, cols = src.shape
    if permuted_cols:
        lane = lax.broadcasted_iota(jnp.int32, (rows, LANES), 1)
        shift = HEAD_DIM - _HALF_ROT
        to_b_r1 = (lane >= _HALF_ROT) & (lane < ROPE_DIM)
        to_a_r2 = (lane >= HEAD_DIM) & (lane < HEAD_DIM + _HALF_ROT)
        for g in range(permuted_cols // LANES):
            lanes = slice(g * LANES, (g + 1) * LANES)
            v = src[:, lanes]
            v = jnp.where(to_b_r1, pltpu.roll(v, LANES - shift, 1), jnp.where(to_a_r2, pltpu.roll(v, shift, 1), v))
            dst[:, lanes] = v.astype(BF16)
    if permuted_cols < cols:
        dst[:, permuted_cols:] = src[:, permuted_cols:].astype(BF16)


def _rope_tables(positions, casts=()):
    bsz, seq = positions.shape
    lane = jnp.arange(LANES)
    inv = ROPE_THETA ** (-jnp.arange(0, ROPE_DIM, 2, dtype=F32) / ROPE_DIM)
    first = lane < ROPE_DIM
    second = (lane >= LANES // 2) & (lane < LANES // 2 + ROPE_DIM)
    rotated = first | second
    expand = (rotated[None, :] & (lane[None, :] % _HALF_ROT == jnp.arange(_HALF_ROT)[:, None])).astype(F32)
    base_lane = jnp.where(rotated, 0.0, 1.0).astype(F32)[None, :]
    sgn_lane = jnp.where(first, -1.0, jnp.where(second, 1.0, 0.0)).astype(F32)[None, :]
    t = min(ROPE_ROWS, seq)
    tab = jax.ShapeDtypeStruct((bsz, seq, LANES), F32)
    jobs = [_cast_job(stacked, index, bsz * (seq // t), seq // t) for stacked, index in casts]
    out = pl.pallas_call(
        functools.partial(_rope_kernel, n_casts=len(jobs)),
        grid=(bsz, seq // t),
        in_specs=[
            pl.BlockSpec((1, 1, t), lambda b, i: (b, 0, i)),
            pl.BlockSpec((_HALF_ROT, 1), lambda b, i: (0, 0)),
            pl.BlockSpec((_HALF_ROT, LANES), lambda b, i: (0, 0)),
            pl.BlockSpec((1, LANES), lambda b, i: (0, 0)),
            pl.BlockSpec((1, LANES), lambda b, i: (0, 0)),
        ] + [src for src, _, _ in jobs],
        out_specs=[pl.BlockSpec((1, t, LANES), lambda b, i: (b, i, 0))] * 2 + [dst for _, dst, _ in jobs],
        out_shape=[tab, tab] + [shape for _, _, shape in jobs],
        compiler_params=_params("arbitrary", "arbitrary"),
        name="rope",
    )(positions.reshape(bsz, 1, seq), inv[:, None], expand, base_lane, sgn_lane, *[stacked for stacked, _ in casts])
    return out[0], out[1], out[2:]


def _apply_rope(t, c, s):
    return t * c + pltpu.roll(t, LANES // 2, 1) * s


def _ffn_kernel(*refs, g_row, mod_off, chunks, mix_gate_off, final_norm, cast_perm):
    x_ref, mods_ref, g_ref, win_ref, wout_ref = refs[:5]
    rest = list(refs[5:])
    a_ref, wo_ref = (rest.pop(0), rest.pop(0)) if mix_gate_off is not None else (None, None)
    fg_ref = rest.pop(0) if final_norm else None
    cast_src = [rest.pop(0) for _ in cast_perm]
    o_ref = rest.pop(0)
    cast_dst = [rest.pop(0) for _ in cast_perm]
    (h_scr,) = rest
    for src, dst, permuted_cols in zip(cast_src, cast_dst, cast_perm):
        _cast_block(src, dst, permuted_cols)
    d = x_ref.shape[-1]
    f = wout_ref.shape[0]
    x = x_ref[0]
    if mix_gate_off is not None:
        mix = jnp.dot(a_ref[0], wo_ref[...], preferred_element_type=F32)
        x = x + (1.0 + _mod(mods_ref, mix_gate_off, d)) * mix
    o_ref[0] = x
    h = _norm_mod(x, g_ref[g_row:g_row + 1, :], _mod(mods_ref, mod_off, d), _mod(mods_ref, mod_off + 1, d))
    h_scr[...] = h.astype(BF16)
    gate = FFN_RES_WEIGHT * (1.0 + _mod(mods_ref, mod_off + 2, d))
    for lo, hi in chunks:
        a = jnp.dot(h_scr[...], win_ref[:, lo:hi], preferred_element_type=F32)
        b = jnp.dot(h_scr[...], win_ref[:, f + lo:f + hi], preferred_element_type=F32)
        act = ((a * jax.nn.sigmoid(a)) * b).astype(BF16)
        o_ref[0] += gate * jnp.dot(act, wout_ref[lo:hi, :], preferred_element_type=F32)
    if final_norm:
        y = o_ref[0]
        ms = jnp.mean(y * y, axis=-1, keepdims=True)
        o_ref[0] = (y * lax.rsqrt(ms + NORM_EPS)) * fg_ref[...]


def _cast_job(stacked, layer, n_steps, tiles):
    rows, cols = stacked.shape[1:]
    n = n_steps
    while rows % (n * 2 * SUBLANES):
        n //= 2
    blk = rows // n
    src = pl.BlockSpec((None, blk, cols), lambda b, i: (layer, jnp.minimum(b * tiles + i, n - 1), 0))
    dst = pl.BlockSpec((blk, cols), lambda b, i: (jnp.minimum(b * tiles + i, n - 1), 0))
    return src, dst, jax.ShapeDtypeStruct((rows, cols), BF16)


def _ffn(x, mods, norm_g, w_in, w_out, layer, w_layer, g_row, mod_off, mix=None, final_g=None, casts=()):
    bsz, seq, d = x.shape
    f = w_out.shape[1]
    chunks = tuple((lo, min(lo + FFN_CHUNK, f)) for lo in range(0, f, FFN_CHUNK))
    tm = min(FFN_ROWS, seq)
    tiles = seq // tm
    in_specs = [
        pl.BlockSpec((1, tm, d), lambda b, i: (b, i, 0)),
        _layer_spec(mods, layer),
        _layer_spec(norm_g, layer),
        _layer_spec(w_in, w_layer),
        _layer_spec(w_out, w_layer),
    ]
    args = [x, mods, norm_g, w_in, w_out]
    mix_gate_off = None
    if mix is not None:
        a, w_o, j, mix_gate_off = mix
        in_specs += [pl.BlockSpec((1, tm, a.shape[-1]), lambda b, i: (b, i, 0)), _layer_spec(w_o, j)]
        args += [a, w_o]
    if final_g is not None:
        in_specs.append(pl.BlockSpec((1, d), lambda b, i: (0, 0)))
        args.append(final_g)
    out_specs = [pl.BlockSpec((1, tm, d), lambda b, i: (b, i, 0))]
    out_shape = [jax.ShapeDtypeStruct(x.shape, F32)]
    for stacked, index, *_ in casts:
        src, dst, shape = _cast_job(stacked, index, bsz * tiles, tiles)
        in_specs.append(src)
        args.append(stacked)
        out_specs.append(dst)
        out_shape.append(shape)
    cast_perm = tuple(job[2] if len(job) > 2 else 0 for job in casts)
    out = pl.pallas_call(
        functools.partial(_ffn_kernel, g_row=g_row, mod_off=mod_off, chunks=chunks, mix_gate_off=mix_gate_off,
                          final_norm=final_g is not None, cast_perm=cast_perm),
        grid=(bsz, tiles),
        in_specs=in_specs,
        out_specs=out_specs,
        out_shape=out_shape,
        scratch_shapes=[pltpu.VMEM((tm, d), BF16)],
        compiler_params=_params("arbitrary", "arbitrary"),
        name="ffn",
    )(*args)
    return out[0], out[1:]


def _conv_kernel(x_ref, mods_ref, g_ref, win_ref, cw_ref, wout_ref, o_ref, h_scr, vbuf, carry,
                 *, g_row, mod_off, n_chunks):
    tm, d = h_scr.shape
    ck = d // n_chunks

    @pl.when(pl.program_id(1) == 0)
    def _():
        carry[...] = jnp.zeros_like(carry)

    h = _norm_mod(x_ref[0], g_ref[g_row:g_row + 1, :], _mod(mods_ref, mod_off, d), _mod(mods_ref, mod_off + 1, d))
    h_scr[...] = h.astype(BF16)
    gate = 1.0 + _mod(mods_ref, mod_off + 2, d)
    for j in range(n_chunks):
        cols = [slice(k * d + j * ck, k * d + (j + 1) * ck) for k in range(3)]
        v = (jnp.dot(h_scr[...], win_ref[:, cols[1]], preferred_element_type=F32)
             * jnp.dot(h_scr[...], win_ref[:, cols[2]], preferred_element_type=F32))
        vbuf[0:SUBLANES, :] = carry[j]
        vbuf[SUBLANES:, :] = v
        carry[j] = vbuf[tm:, :]
        cw = cw_ref[:, j * ck:(j + 1) * ck]
        conv = (cw[0:1] * vbuf[SUBLANES - 2:tm + SUBLANES - 2, :]
                + cw[1:2] * vbuf[SUBLANES - 1:tm + SUBLANES - 1, :]
                + cw[2:3] * v)
        bg = jnp.dot(h_scr[...], win_ref[:, cols[0]], preferred_element_type=F32)
        y = jnp.dot((bg * conv).astype(BF16), wout_ref[j * ck:(j + 1) * ck, :], preferred_element_type=F32)
        if j == 0:
            o_ref[0] = x_ref[0] + gate * y
        else:
            o_ref[0] += gate * y


def _conv(x, mods, norm_g, w_in, conv_w, w_out, layer, w_layer, g_row, mod_off):
    bsz, seq, d = x.shape
    ck = CONV_CHUNK
    nc = d // ck
    tm = min(CONV_ROWS, seq)
    return pl.pallas_call(
        functools.partial(_conv_kernel, g_row=g_row, mod_off=mod_off, n_chunks=nc),
        grid=(bsz, seq // tm),
        in_specs=[
            pl.BlockSpec((1, tm, d), lambda b, i: (b, i, 0)),
            _layer_spec(mods, layer),
            _layer_spec(norm_g, layer),
            _layer_spec(w_in, w_layer),
            _layer_spec(conv_w, layer),
            _layer_spec(w_out, w_layer),
        ],
        out_specs=pl.BlockSpec((1, tm, d), lambda b, i: (b, i, 0)),
        out_shape=jax.ShapeDtypeStruct(x.shape, F32),
        scratch_shapes=[
            pltpu.VMEM((tm, d), BF16),
            pltpu.VMEM((tm + SUBLANES, ck), F32),
            pltpu.VMEM((nc, SUBLANES, ck), F32),
        ],
        compiler_params=_params("arbitrary", "arbitrary"),
        name="conv",
    )(x, mods, norm_g, w_in, conv_w, w_out)


def _proj_kernel(*refs, g_row, shift_off, plan):
    x_ref, mods_ref, g_ref, w_ref, c_ref, s_ref = refs[:6]
    outs = refs[6:6 + len(plan)]
    zs, zs2, h_scr = refs[6 + len(plan):]
    tm, d = x_ref.shape[1:]
    h = _norm_mod(x_ref[0], g_ref[g_row:g_row + 1, :], _mod(mods_ref, shift_off, d),
                  _mod(mods_ref, shift_off + 1, d))
    h_scr[...] = h.astype(BF16)
    slices_per_dot = PROJ_DOT_COLS // LANES
    for j in sorted(range(len(plan)), key=lambda n: plan[n][:2], reverse=True):
        dil, rope, mult = plan[j]
        for half in range(GROUP_WIDTH // PROJ_DOT_COLS):
            col0 = j * GROUP_WIDTH + half * PROJ_DOT_COLS
            z = jnp.dot(h_scr[...], w_ref[:, col0:col0 + PROJ_DOT_COLS], preferred_element_type=F32)
            for k in range(slices_per_dot):
                plane = half * slices_per_dot + k
                lanes = slice(plane * LANES, (plane + 1) * LANES)
                zk = z[:, k * LANES:(k + 1) * LANES]
                if rope:
                    zk = _apply_rope(zk, c_ref[0], s_ref[0])
                if mult != 1.0:
                    zk = zk * mult
                if dil == 1:
                    outs[j][0, 0, :, lanes] = zk.astype(BF16)
                else:
                    zs[plane] = zk
                    if dil <= RELAYOUT_STRIDE:
                        for r in range(dil):
                            outs[j][0, r, :, lanes] = zs[plane, pl.ds(r, tm // dil, stride=dil), :].astype(BF16)
                    else:
                        s1, s2 = RELAYOUT_STRIDE, dil // RELAYOUT_STRIDE
                        run = tm // s1
                        for r1 in range(s1):
                            zs2[plane, r1 * run:(r1 + 1) * run, :] = zs[plane, pl.ds(r1, run, stride=s1), :]
                        for r in range(dil):
                            r1, r2 = r % s1, r // s1
                            rows = zs2[plane, pl.ds(r1 * run + r2, tm // dil, stride=s2), :]
                            outs[j][0, r, :, lanes] = rows.astype(BF16)


def _proj(x, mods, norm_g, w, ctab, stab, layer, g_row, w_index, shift_off, plan):
    bsz, seq, d = x.shape
    tm = min(PROJ_ROWS, seq)
    out_shape = [jax.ShapeDtypeStruct((bsz, dil, seq // dil, GROUP_WIDTH), BF16) for dil, _, _ in plan]
    out_specs = [pl.BlockSpec((1, dil, tm // dil, GROUP_WIDTH), lambda b, i: (b, 0, i, 0)) for dil, _, _ in plan]
    return pl.pallas_call(
        functools.partial(_proj_kernel, g_row=g_row, shift_off=shift_off, plan=plan),
        grid=(bsz, seq // tm),
        in_specs=[
            pl.BlockSpec((1, tm, d), lambda b, i: (b, i, 0)),
            _layer_spec(mods, layer),
            _layer_spec(norm_g, layer),
            _layer_spec(w, w_index),
            pl.BlockSpec((1, tm, LANES), lambda b, i: (b, i, 0)),
            pl.BlockSpec((1, tm, LANES), lambda b, i: (b, i, 0)),
        ],
        out_specs=out_specs,
        out_shape=out_shape,
        scratch_shapes=[pltpu.VMEM((GROUP_WIDTH // LANES, tm, LANES), F32)] * 2 + [pltpu.VMEM((tm, d), BF16)],
        compiler_params=_params("parallel", "parallel"),
        name="proj",
    )(x, mods, norm_g, w, ctab, stab)


def _attn_kernel(*refs, dils, rows):
    ng = len(dils)
    ins = refs[:5 * ng]
    o_ref = refs[5 * ng]
    scr = refs[5 * ng + 1:]
    *strided, dense = sorted(range(ng), key=lambda g: dils[g], reverse=True)
    assert dils[dense] == 1
    accs = dict(zip(strided, scr[0:ng - 1]))
    mxs = dict(zip(strided, scr[ng - 1:2 * ng - 2]))
    dens = dict(zip(strided, scr[2 * ng - 2:3 * ng - 3]))
    bias_ref = scr[3 * ng - 3]
    at_start = (pl.program_id(1) == 0).astype(jnp.int32)

    row = lax.broadcasted_iota(jnp.int32, (2 * BAND, 2 * BAND), 0) % BAND
    col = lax.broadcasted_iota(jnp.int32, (2 * BAND, 2 * BAND), 1)
    valid = (col >= row) & (col <= row + BAND)
    bias_ref[0] = jnp.where(valid, 0.0, -jnp.inf)
    bias_ref[1] = jnp.where(valid & (col >= BAND), 0.0, -jnp.inf)
    lane = lax.broadcasted_iota(jnp.int32, (BAND, LANES), 1)
    q_first = _first_head_lanes(lane)
    lo = lane < HEAD_DIM
    ones = jnp.ones((2 * BAND, LANES), BF16)

    def block_stats(g, r, j):
        q_ref, k_ref, kh_ref, v_ref, vh_ref = ins[5 * g:5 * g + 5]
        cur = slice(j * BAND, (j + 1) * BAND)
        q = q_ref[0, r, cur, :]
        zero = jnp.zeros_like(q)
        q2 = jnp.concatenate([jnp.where(q_first, q, zero), jnp.where(q_first, zero, q)], axis=0)
        if j == 0:
            k = jnp.concatenate([kh_ref[0, r], k_ref[0, r, cur, :]], axis=0)
            v = jnp.concatenate([vh_ref[0, r], v_ref[0, r, cur, :]], axis=0)
            bias = bias_ref[at_start]
        else:
            both = slice((j - 1) * BAND, (j + 1) * BAND)
            k = k_ref[0, r, both, :]
            v = v_ref[0, r, both, :]
            bias = bias_ref[0]
        s = lax.dot_general(q2, k, (((1,), (1,)), ((), ())), preferred_element_type=F32) + bias
        m = jnp.max(s, axis=-1, keepdims=True)
        p = jnp.exp2(s - m).astype(BF16)
        pv = jnp.dot(p, jnp.concatenate([v, ones], axis=1), preferred_element_type=F32)
        return (jnp.where(lo, pv[:BAND, :LANES], pv[BAND:, :LANES]), jnp.where(lo, m[:BAND], m[BAND:]),
                jnp.where(lo, pv[:BAND, LANES:], pv[BAND:, LANES:]))

    for g in strided:
        dil = dils[g]
        for r in range(dil):
            for j in range(rows // dil // BAND):
                idx = pl.ds(j * BAND * dil + r, BAND, stride=dil)
                accs[g][idx, :], mxs[g][idx, :], dens[g][idx, :] = block_stats(g, r, j)
    for j in range(rows // BAND):
        sl = slice(j * BAND, (j + 1) * BAND)
        parts = [block_stats(dense, 0, j)] + [(accs[g][sl, :], mxs[g][sl, :], dens[g][sl, :]) for g in strided]
        m_all = functools.reduce(jnp.maximum, [m for _, m, _ in parts])
        num = den = None
        for acc, m, l in parts:
            a = jnp.exp2(m - m_all)
            num = a * acc if num is None else num + a * acc
            den = a * l if den is None else den + a * l
        o_ref[0, sl, :] = (num / den).astype(BF16)


def _attn(qs, ks, vs):
    bsz = qs[0].shape[0]
    dils = tuple(q.shape[1] for q in qs)
    seq = qs[0].shape[1] * qs[0].shape[2]
    rows = min(ATTN_ROWS, seq)
    n_pairs = GROUP_WIDTH // LANES
    in_specs, args = [], []
    for q, k, v in zip(qs, ks, vs):
        dil = q.shape[1]
        blk = rows // dil
        per = blk // BAND
        main = pl.BlockSpec((1, dil, blk, LANES), lambda b, i, hp: (b, 0, i, hp))
        halo = pl.BlockSpec((1, dil, BAND, LANES),
                            lambda b, i, hp, per=per: (b, 0, jnp.maximum(i * per - 1, 0), hp))
        in_specs += [main, main, halo, main, halo]
        args += [q, k, k, v, v]
    stat = [pltpu.VMEM((rows, LANES), F32)] * (3 * (len(dils) - 1))
    return pl.pallas_call(
        functools.partial(_attn_kernel, dils=dils, rows=rows),
        grid=(bsz, seq // rows, n_pairs),
        in_specs=in_specs,
        out_specs=pl.BlockSpec((1, rows, LANES), lambda b, i, hp: (b, i, hp)),
        out_shape=jax.ShapeDtypeStruct((bsz, seq, GROUP_WIDTH), BF16),
        scratch_shapes=stat + [pltpu.VMEM((2, 2 * BAND, 2 * BAND), F32)],
        compiler_params=_params("parallel", "parallel", "parallel"),
        name="attn",
    )(*args)


def kernel(x, c, positions, norm_g, ada_w, ada_b, ffn1_w_in, ffn1_w_out, ffn2_w_in, ffn2_w_out, conv_w_in, conv_w,
           conv_w_out, kv_norm_g, kv_ada_w, kv_ada_b, w_kv, attn_w_q, attn_w_o, final_norm_g):
    bsz, seq, d = x.shape
    depth = ada_w.shape[0]
    n_conv = conv_w_in.shape[0]
    assert all(win // dil == BAND for win, dil in DILATED_GROUPS)
    assert bsz <= SUBLANES and seq % ATTN_ROWS == 0
    dils = tuple(dil for _, dil in DILATED_GROUPS)

    c8 = jnp.zeros((SUBLANES, d), F32).at[:bsz].set(c)
    mods = _ada(c8, ada_w, ada_b, ADA_COLS)
    kv_mods = _ada(c8, kv_ada_w[None], kv_ada_b[None], d)
    ctab, stab, (w_in, w_out) = _rope_tables(positions, casts=[(ffn1_w_in, 0), (ffn1_w_out, 0)])
    w_in, w_out = w_in[None], w_out[None]

    attn_casts = [(w_kv[None], 0, w_kv.shape[1] // 2)]
    attn_casts += [(attn_w_q, j, attn_w_q.shape[2]) for j in range(depth - n_conv)]
    attn_casts += [(attn_w_o, j) for j in range(depth - n_conv)]
    assert n_conv >= 1
    w_kv_p = w_q = w_o = None

    kv_plan = tuple((dil, True, 1.0) for dil in dils) + tuple((dil, False, 1.0) for dil in dils)
    q_plan = tuple((dil, True, LOG2_E * HEAD_DIM ** -0.5) for dil in dils)
    ks = vs = None
    for layer in range(depth):
        if layer == n_conv:
            kv = _proj(x, kv_mods, kv_norm_g[None, None], w_kv_p[None], ctab, stab, 0, 0, 0, 0, kv_plan)
            ks, vs = kv[:len(dils)], kv[len(dils):]
        casts = [(ffn2_w_in, layer), (ffn2_w_out, layer)]
        if layer < n_conv:
            casts += [(conv_w_in, layer), (conv_w_out, layer)]
        x, cast = _ffn(x, mods, norm_g, w_in, w_out, layer, 0, 0, 0, casts=casts)
        w_in, w_out = cast[0][None], cast[1][None]
        mix = None
        if layer < n_conv:
            x = _conv(x, mods, norm_g, cast[2][None], conv_w, cast[3][None], layer, 0, 1, 3)
        else:
            j = layer - n_conv
            qs = _proj(x, mods, norm_g, w_q[j][None], ctab, stab, layer, 1, 0, 3, q_plan)
            mix = (_attn(qs, ks, vs), w_o[j][None], 0, 5)
        last = layer == depth - 1
        casts = [] if last else [(ffn1_w_in, layer + 1), (ffn1_w_out, layer + 1)]
        if layer == n_conv - 1:
            casts += attn_casts
        x, cast = _ffn(x, mods, norm_g, w_in, w_out, layer, 0, 2, 6, mix=mix,
                       final_g=final_norm_g[None] if last else None, casts=casts)
        if not last:
            w_in, w_out = cast[0][None], cast[1][None]
        if layer == n_conv - 1:
            n_attn = depth - n_conv
            w_kv_p, w_q, w_o = cast[2], cast[3:3 + n_attn], cast[3 + n_attn:3 + 2 * n_attn]
    return x
```
